```python
import math
import jax, jax.numpy as jnp
from jax import lax
import numpy as np

D_MODEL = 4096
BATCH = 4
SEQ = 2048
DEPTH = 1
DEC_BATCH = 128
DEC_SEQ = 4
PAST_LEN = 16384
PAGE_SIZE = 128

N_META = 16
N_HEADS = 16
HEAD_DIM = 128
DN_WIDTH = N_HEADS * HEAD_DIM
QKV_WIDTH = 3 * DN_WIDTH
SHORT_CONV = 4
CHUNK = 64
CONF_WIDTH = D_MODEL // 2
CONF_KERNEL = 31
N_GROUPS = 8
EXPERTS_PER_GROUP = 8
N_EXPERTS = N_GROUPS * EXPERTS_PER_GROUP
TOP_K = 2
D_EXPERT = D_MODEL // 4
EPS = 1e-6
IN_SIZES = (QKV_WIDTH, DN_WIDTH, N_HEADS, N_HEADS, 2 * CONF_WIDTH, 2 * D_MODEL)
IN_WIDTH = sum(IN_SIZES)

kernel_name = "hybrid_gdn_conformer_hmoe_step"


def rmsnorm(x, w):
    xf = x.astype(jnp.float32)
    y = xf * lax.rsqrt(jnp.mean(xf * xf, axis=-1, keepdims=True) + EPS)
    return (y * w.astype(jnp.float32)).astype(x.dtype)


def layernorm(x, w, b):
    xf = x.astype(jnp.float32)
    mu = jnp.mean(xf, axis=-1, keepdims=True)
    xc = xf - mu
    y = xc * lax.rsqrt(jnp.mean(xc * xc, axis=-1, keepdims=True) + EPS)
    return (y * w.astype(jnp.float32) + b.astype(jnp.float32)).astype(x.dtype)


def l2norm(x):
    return x * lax.rsqrt(jnp.sum(x * x, axis=-1, keepdims=True) + EPS)


def causal_dwconv(buf, w):
    return lax.conv_general_dilated(buf, w[:, None, :].astype(buf.dtype), window_strides=(1,), padding='VALID',
                                    dimension_numbers=('NWC', 'WIO', 'NWC'), feature_group_count=w.shape[1])


def delta_chunk(S, inp):
    q, k, v, g, beta = inp
    C = q.shape[-2]
    idx = jnp.arange(C)
    causal = idx[:, None] >= idx[None, :]
    strict = idx[:, None] > idx[None, :]
    G = jnp.cumsum(g, axis=-1)
    decay = jnp.exp(jnp.where(causal, G[..., :, None] - G[..., None, :], -jnp.inf))
    kb = k * beta[..., None]
    lower = jnp.where(strict, jnp.einsum('nhid,nhjd->nhij', kb, k) * decay, 0.0)
    eye = jnp.eye(C, dtype=q.dtype)
    rhs = jnp.concatenate([v * beta[..., None], kb * jnp.exp(G)[..., None]], axis=-1)
    sol = lax.linalg.triangular_solve(eye + lower, rhs, left_side=True, lower=True, unit_diagonal=True)
    dv = v.shape[-1]
    u = sol[..., :dv] - jnp.einsum('nhik,nhkv->nhiv', sol[..., dv:], S)
    attn = jnp.einsum('nhid,nhjd->nhij', q, k) * decay
    o = jnp.einsum('nhik,nhkv->nhiv', q * jnp.exp(G)[..., None], S) + jnp.einsum('nhij,nhjv->nhiv', attn, u)
    G_last = G[..., -1:]
    S_new = S * jnp.exp(G_last)[..., None] + jnp.einsum('nhik,nhiv->nhkv', k * jnp.exp(G_last - G)[..., None], u)
    return S_new, o


def gated_delta(q, k, v, g, beta, S0, segments):
    N, H = q.shape[0], q.shape[1]
    S = S0.astype(jnp.float32)
    outs = []
    start = 0
    for length, chunk in segments:
        n = length // chunk

        def split(t):
            t = t[:, :, start:start + length].astype(jnp.float32)
            t = t.reshape(t.shape[:2] + (n, chunk) + t.shape[3:])
            return jnp.moveaxis(t, 2, 0)

        S, o = lax.scan(delta_chunk, S, (split(q), split(k), split(v), split(g), split(beta)))
        outs.append(jnp.moveaxis(o, 0, 2).reshape(N, H, length, v.shape[-1]))
        start += length
    return jnp.concatenate(outs, axis=2), S


def token_mixer(hn, conv_qkv_prev, conv_b_prev, S0, segments,
                w_in, w_conv_qkv, a_log, dt_bias, w_onorm, w_proj_a,
                w_dw, b_dw, ln_w, ln_b, w_proj_b, b_proj_b, w_out):
    N, L, _ = hn.shape
    proj = hn @ w_in
    qkv_raw, z, a, b, glu_in, gate_logits = jnp.split(proj, list(np.cumsum(IN_SIZES)[:-1]), axis=-1)
    qkv_buf = jnp.concatenate([conv_qkv_prev.astype(hn.dtype), qkv_raw], axis=1)
    qkv = jax.nn.silu(causal_dwconv(qkv_buf, w_conv_qkv))
    new_conv_qkv = qkv_buf[:, -(SHORT_CONV - 1):]
    q, k, v = [t.reshape(N, L, N_HEADS, HEAD_DIM).astype(jnp.float32) for t in jnp.split(qkv, 3, axis=-1)]
    q = l2norm(q) * (HEAD_DIM ** -0.5)
    k = l2norm(k)
    g = -jnp.exp(a_log.astype(jnp.float32)) * jax.nn.softplus(a.astype(jnp.float32) + dt_bias.astype(jnp.float32))
    beta = jax.nn.sigmoid(b.astype(jnp.float32))
    o, S = gated_delta(jnp.swapaxes(q, 1, 2), jnp.swapaxes(k, 1, 2), jnp.swapaxes(v, 1, 2),
                       jnp.swapaxes(g, 1, 2), jnp.swapaxes(beta, 1, 2), S0, segments)
    o = jnp.swapaxes(o, 1, 2).astype(hn.dtype)
    o = rmsnorm(o, w_onorm) * jax.nn.silu(z.reshape(N, L, N_HEADS, HEAD_DIM))
    y_a = o.reshape(N, L, DN_WIDTH) @ w_proj_a
    u, ug = jnp.split(glu_in, 2, axis=-1)
    glu = u * jax.nn.sigmoid(ug)
    cb_buf = jnp.concatenate([conv_b_prev.astype(hn.dtype), glu], axis=1)
    c = causal_dwconv(cb_buf, w_dw) + b_dw
    new_conv_b = cb_buf[:, -(CONF_KERNEL - 1):]
    c = jax.nn.silu(layernorm(c, ln_w, ln_b))
    y_b = c @ w_proj_b + b_proj_b
    ga, gb = jnp.split(jax.nn.sigmoid(gate_logits), 2, axis=-1)
    out = (ga * y_a + gb * y_b) @ w_out
    return out, S.astype(S0.dtype), new_conv_qkv, new_conv_b


def hier_moe(xn, w_rg, b_rg, w_re, b_re, w_g, w_u, w_d):
    N, L, D = xn.shape
    t = xn.reshape(-1, D)
    T = t.shape[0]
    pg = jax.nn.softmax((t @ w_rg + b_rg).astype(jnp.float32), axis=-1)
    pg_top, gidx = lax.top_k(pg, 1)
    el = (t @ w_re + b_re).astype(jnp.float32).reshape(T, N_GROUPS, EXPERTS_PER_GROUP)
    el_g = jnp.take_along_axis(el, gidx[:, :, None], axis=1)[:, 0]
    pe_top, eidx = lax.top_k(jax.nn.softmax(el_g, axis=-1), TOP_K)
    wts = pg_top * pe_top / jnp.sum(pe_top, axis=-1, keepdims=True)
    flat_e = (gidx * EXPERTS_PER_GROUP + eidx).reshape(-1)
    flat_w = wts.reshape(-1)
    flat_tok = jnp.repeat(jnp.arange(T), TOP_K)
    order = jnp.argsort(flat_e)
    tok_s = flat_tok[order]
    xs = t[tok_s]
    gs = jnp.bincount(flat_e, length=N_EXPERTS).astype(jnp.int32)
    h = jax.nn.silu(lax.ragged_dot(xs, w_g, gs)) * lax.ragged_dot(xs, w_u, gs)
    out = lax.ragged_dot(h, w_d, gs) * flat_w[order][:, None].astype(t.dtype)
    y = jnp.zeros_like(t).at[tok_s].add(out)
    return y.reshape(N, L, D)


def setup_inputs(seed: int = 0) -> dict:
    key = jax.random.key(seed)
    ks = iter(jax.random.split(key, 32))
    f32 = jnp.float32

    def nrm(shape, scale):
        return jax.random.normal(next(ks), shape, f32) * scale

    def gain(shape):
        return 1.0 + nrm(shape, 0.02)

    x_prompt = nrm((BATCH, SEQ, D_MODEL), 1.0)
    x_sample = nrm((DEC_BATCH, DEC_SEQ, D_MODEL), 1.0)
    state_delta = nrm((DEPTH, DEC_BATCH, N_HEADS, HEAD_DIM, HEAD_DIM), HEAD_DIM ** -0.5)
    state_conv_qkv = nrm((DEPTH, DEC_BATCH, SHORT_CONV - 1, QKV_WIDTH), 1.0)
    state_conv_b = nrm((DEPTH, DEC_BATCH, CONF_KERNEL - 1, CONF_WIDTH), 0.5)
    meta_tokens = nrm((N_META, D_MODEL), 1.0)
    norm1 = gain((DEPTH, D_MODEL))
    w_in = nrm((DEPTH, D_MODEL, IN_WIDTH), D_MODEL ** -0.5)
    w_conv_qkv = nrm((DEPTH, SHORT_CONV, QKV_WIDTH), SHORT_CONV ** -0.5)
    a_log = jnp.log(jax.random.uniform(next(ks), (DEPTH, N_HEADS), f32, 1.0, 16.0))
    dt = jnp.exp(jax.random.uniform(next(ks), (DEPTH, N_HEADS), f32, math.log(1e-3), math.log(1e-1)))
    dt_bias = dt + jnp.log(-jnp.expm1(-dt))
    w_onorm = gain((DEPTH, HEAD_DIM))
    w_proj_a = nrm((DEPTH, DN_WIDTH, D_MODEL), DN_WIDTH ** -0.5)
    w_dw = nrm((DEPTH, CONF_KERNEL, CONF_WIDTH), CONF_KERNEL ** -0.5)
    b_dw = nrm((DEPTH, CONF_WIDTH), 0.02)
    ln_w = gain((DEPTH, CONF_WIDTH))
    ln_b = nrm((DEPTH, CONF_WIDTH), 0.02)
    w_proj_b = nrm((DEPTH, CONF_WIDTH, D_MODEL), CONF_WIDTH ** -0.5)
    b_proj_b = nrm((DEPTH, D_MODEL), 0.02)
    w_out = nrm((DEPTH, D_MODEL, D_MODEL), D_MODEL ** -0.5)
    norm2 = gain((DEPTH, D_MODEL))
    w_rg = nrm((DEPTH, D_MODEL, N_GROUPS), D_MODEL ** -0.5)
    b_rg = nrm((DEPTH, N_GROUPS), 0.01)
    w_re = nrm((DEPTH, D_MODEL, N_EXPERTS), D_MODEL ** -0.5)
    b_re = nrm((DEPTH, N_EXPERTS), 0.01)
    w_e_gate = nrm((DEPTH, N_EXPERTS, D_MODEL, D_EXPERT), D_MODEL ** -0.5)
    w_e_up = nrm((DEPTH, N_EXPERTS, D_MODEL, D_EXPERT), D_MODEL ** -0.5)
    w_e_down = nrm((DEPTH, N_EXPERTS, D_EXPERT, D_MODEL), D_EXPERT ** -0.5)
    final_norm = gain((D_MODEL,))
    return {"x_prompt": x_prompt, "x_sample": x_sample, "state_delta": state_delta,
            "state_conv_qkv": state_conv_qkv, "state_conv_b": state_conv_b, "meta_tokens": meta_tokens,
            "norm1": norm1, "w_in": w_in, "w_conv_qkv": w_conv_qkv, "a_log": a_log, "dt_bias": dt_bias,
            "w_onorm": w_onorm, "w_proj_a": w_proj_a, "w_dw": w_dw, "b_dw": b_dw, "ln_w": ln_w, "ln_b": ln_b,
            "w_proj_b": w_proj_b, "b_proj_b": b_proj_b, "w_out": w_out, "norm2": norm2,
            "w_rg": w_rg, "b_rg": b_rg, "w_re": w_re, "b_re": b_re,
            "w_e_gate": w_e_gate, "w_e_up": w_e_up, "w_e_down": w_e_down, "final_norm": final_norm}


def reference(x_prompt, x_sample, state_delta, state_conv_qkv, state_conv_b, meta_tokens,
              norm1, w_in, w_conv_qkv, a_log, dt_bias, w_onorm, w_proj_a, w_dw, b_dw, ln_w, ln_b,
              w_proj_b, b_proj_b, w_out, norm2, w_rg, b_rg, w_re, b_re, w_e_gate, w_e_up, w_e_down,
              final_norm):
    B, S_len, D = x_prompt.shape
    dec_seq = x_sample.shape[1]
    segs_prompt = ((N_META, N_META), (S_len, min(CHUNK, S_len)))
    segs_sample = ((dec_seq, dec_seq),)
    hp = jnp.concatenate([jnp.broadcast_to(meta_tokens.astype(x_prompt.dtype)[None], (B, N_META, D)), x_prompt], axis=1)
    hs = x_sample
    nd_p, ncq_p, ncb_p, nd_s, ncq_s, ncb_s = [], [], [], [], [], []
    for l in range(DEPTH):
        mix_p = (w_in[l], w_conv_qkv[l], a_log[l], dt_bias[l], w_onorm[l], w_proj_a[l],
                 w_dw[l], b_dw[l], ln_w[l], ln_b[l], w_proj_b[l], b_proj_b[l], w_out[l])
        zq = jnp.zeros((B, SHORT_CONV - 1, QKV_WIDTH), hp.dtype)
        zb = jnp.zeros((B, CONF_KERNEL - 1, CONF_WIDTH), hp.dtype)
        zS = jnp.zeros((B, N_HEADS, HEAD_DIM, HEAD_DIM), hp.dtype)
        mp, Sp, cqp, cbp = token_mixer(rmsnorm(hp, norm1[l]), zq, zb, zS, segs_prompt, *mix_p)
        ms, Ss, cqs, cbs = token_mixer(rmsnorm(hs, norm1[l]), state_conv_qkv[l], state_conv_b[l],
                                       state_delta[l], segs_sample, *mix_p)
        hp = hp + mp
        hs = hs + ms
        moe_p = (w_rg[l], b_rg[l], w_re[l], b_re[l], w_e_gate[l], w_e_up[l], w_e_down[l])
        hp = hp + hier_moe(rmsnorm(hp, norm2[l]), *moe_p)
        hs = hs + hier_moe(rmsnorm(hs, norm2[l]), *moe_p)
        nd_p.append(Sp); ncq_p.append(cqp); ncb_p.append(cbp)
        nd_s.append(Ss); ncq_s.append(cqs); ncb_s.append(cbs)
    y_prompt = rmsnorm(hp, final_norm)[:, N_META:]
    y_sample = rmsnorm(hs, final_norm)
    return (y_prompt, y_sample,
            jnp.stack(nd_p), jnp.stack(ncq_p), jnp.stack(ncb_p),
            jnp.stack(nd_s), jnp.stack(ncq_s), jnp.stack(ncb_s))
```

```python
import functools

import jax
import jax.numpy as jnp
from jax import lax
from jax.experimental import pallas as pl
from jax.experimental.pallas import tpu as pltpu

F32 = jnp.float32
BF16 = jnp.bfloat16
EPS = 1e-6

N_HEADS = 16
HEAD_DIM = 128
DN_WIDTH = N_HEADS * HEAD_DIM
SHORT_CONV = 4
CONF_KERNEL = 31
N_GROUPS = 8
EXPERTS_PER_GROUP = 8
N_EXPERTS = N_GROUPS * EXPERTS_PER_GROUP
CHUNK = 64
LANES = 128
HIGHEST = lax.Precision.HIGHEST

TM = 1024
TN = 512
GROUP_ROWS = 320
EXPERT_F_TILE = 256
VMEM_LIMIT = 56 * 1024 * 1024


def _cp(*sem):
    return pltpu.CompilerParams(dimension_semantics=sem, vmem_limit_bytes=VMEM_LIMIT)


def _sigmoid(x):
    return 1.0 / (1.0 + jnp.exp(-x))


def _silu(x):
    return x * _sigmoid(x)


def _softplus(x):
    return jnp.maximum(x, 0.0) + jnp.log1p(jnp.exp(-jnp.abs(x)))


def _bdot(a, b):
    return jnp.dot(a.astype(BF16), b.astype(BF16), preferred_element_type=F32)


def _rmsnorm_kernel(x_ref, w_ref, o_ref):
    x = x_ref[...]
    ms = jnp.mean(x * x, axis=-1, keepdims=True)
    o_ref[...] = (x * lax.rsqrt(ms + EPS) * w_ref[...]).astype(o_ref.dtype)


def _rmsnorm_bf16(x, w, rows=256):
    t, d = x.shape
    return pl.pallas_call(
        _rmsnorm_kernel,
        grid=(pl.cdiv(t, rows),),
        in_specs=[pl.BlockSpec((rows, d), lambda i: (i, 0)), pl.BlockSpec((1, d), lambda i: (0, 0))],
        out_specs=pl.BlockSpec((rows, d), lambda i: (i, 0)),
        out_shape=jax.ShapeDtypeStruct((t, d), BF16),
        compiler_params=_cp("parallel"),
        name="rmsnorm1",
    )(x, w.reshape(1, d))


def _mm_plain_kernel(x_ref, w_ref, o_ref):
    o_ref[...] = jnp.dot(x_ref[...], w_ref[...], preferred_element_type=F32).astype(o_ref.dtype)


def _mm_sigmoid_kernel(x_ref, w_ref, o_ref):
    o_ref[...] = _sigmoid(jnp.dot(x_ref[...], w_ref[...], preferred_element_type=F32)).astype(o_ref.dtype)


def _mm_glu_kernel(x_ref, wu_ref, wg_ref, o_ref):
    x = x_ref[...]
    u = jnp.dot(x, wu_ref[...], preferred_element_type=F32)
    g = jnp.dot(x, wg_ref[...], preferred_element_type=F32)
    o_ref[...] = u * _sigmoid(g)


def _mm_decay_kernel(x_ref, w_ref, alog_ref, dtb_ref, o_ref):
    acc = jnp.dot(x_ref[...], w_ref[...], preferred_element_type=F32)
    lane = lax.broadcasted_iota(jnp.int32, acc.shape, 1)
    g = -jnp.exp(alog_ref[...]) * _softplus(acc + dtb_ref[...])
    beta = _sigmoid(acc)
    o_ref[...] = jnp.where(lane < N_HEADS, g, jnp.where(lane < 2 * N_HEADS, beta, 0.0))


def _proj(kernel_fn, hn, w, n_out, out_dtype, name, tn=TN):
    t, k = hn.shape
    return pl.pallas_call(
        kernel_fn,
        grid=(pl.cdiv(t, TM), n_out // tn),
        in_specs=[pl.BlockSpec((TM, k), lambda i, j: (i, 0)), pl.BlockSpec((k, tn), lambda i, j: (0, j))],
        out_specs=pl.BlockSpec((TM, tn), lambda i, j: (i, j)),
        out_shape=jax.ShapeDtypeStruct((t, n_out), out_dtype),
        compiler_params=_cp("parallel", "arbitrary"),
        name=name,
    )(hn, w)


def _proj_glu(hn, w_glu):
    t, k = hn.shape
    half = w_glu.shape[1] // 2
    nb = half // TN
    return pl.pallas_call(
        _mm_glu_kernel,
        grid=(pl.cdiv(t, TM), nb),
        in_specs=[
            pl.BlockSpec((TM, k), lambda i, j: (i, 0)),
            pl.BlockSpec((k, TN), lambda i, j: (0, j)),
            pl.BlockSpec((k, TN), lambda i, j: (0, j + nb)),
        ],
        out_specs=pl.BlockSpec((TM, TN), lambda i, j: (i, j)),
        out_shape=jax.ShapeDtypeStruct((t, half), F32),
        compiler_params=_cp("parallel", "arbitrary"),
        name="proj_glu",
    )(hn, w_glu, w_glu)


def _proj_decay(hn, w_ab, alog_pad, dtb_pad):
    t, k = hn.shape
    return pl.pallas_call(
        _mm_decay_kernel,
        grid=(pl.cdiv(t, TM),),
        in_specs=[
            pl.BlockSpec((TM, k), lambda i: (i, 0)),
            pl.BlockSpec((k, LANES), lambda i: (0, 0)),
            pl.BlockSpec((1, LANES), lambda i: (0, 0)),
            pl.BlockSpec((1, LANES), lambda i: (0, 0)),
        ],
        out_specs=pl.BlockSpec((TM, LANES), lambda i: (i, 0)),
        out_shape=jax.ShapeDtypeStruct((t, LANES), F32),
        compiler_params=_cp("parallel"),
        name="proj_decay",
    )(hn, w_ab, alog_pad, dtb_pad)


def _head_normalize(y, sec):
    outs = []
    for h in range(N_HEADS):
        yh = y[:, h * HEAD_DIM:(h + 1) * HEAD_DIM]
        inv = lax.rsqrt(jnp.sum(yh * yh, axis=-1, keepdims=True) + EPS)
        scale = jnp.where(sec == 0, inv * (HEAD_DIM ** -0.5), jnp.where(sec == 1, inv, 1.0))
        outs.append(yh * scale)
    return jnp.concatenate(outs, axis=-1)


def _prep_long_kernel(cur_ref, halo_ref, w_ref, o_ref, ext_ref, *, rows, zero_halo):
    sec = pl.program_id(1)
    halo = halo_ref[...]
    ext_ref[0:8, :] = jnp.zeros_like(halo) if zero_halo else halo
    ext_ref[8:, :] = cur_ref[...]
    acc = w_ref[SHORT_CONV - 1:SHORT_CONV, :] * cur_ref[...]
    for s in range(1, SHORT_CONV):
        acc = acc + w_ref[SHORT_CONV - 1 - s:SHORT_CONV - s, :] * ext_ref[pl.ds(8 - s, rows), :]
    o_ref[...] = _head_normalize(_silu(acc), sec)


def _prep_long(qkvz, w_conv, n_rows, rows, row_block0, halo_index, zero_halo, name):
    nblk = n_rows // rows
    kern = functools.partial(_prep_long_kernel, rows=rows, zero_halo=zero_halo)
    return pl.pallas_call(
        kern,
        grid=(nblk, 3),
        in_specs=[
            pl.BlockSpec((rows, DN_WIDTH), lambda i, s: (i + row_block0, s)),
            pl.BlockSpec((8, DN_WIDTH), lambda i, s: (halo_index(i), s)),
            pl.BlockSpec((SHORT_CONV, DN_WIDTH), lambda i, s: (0, s)),
        ],
        out_specs=pl.BlockSpec((rows, DN_WIDTH), lambda i, s: (i, s)),
        out_shape=jax.ShapeDtypeStruct((n_rows, 3 * DN_WIDTH), F32),
        scratch_shapes=[pltpu.VMEM((rows + 8, DN_WIDTH), F32)],
        compiler_params=_cp("parallel", "arbitrary"),
        name=name,
    )(qkvz, qkvz, w_conv)


def _prep_short_kernel(x_ref, st_ref, w_ref, o_ref, *, steps):
    sec = pl.program_id(0)
    buf = [st_ref[i] for i in range(SHORT_CONV - 1)] + [x_ref[t] for t in range(steps)]
    for t in range(steps):
        acc = w_ref[0:1, :] * buf[t]
        for i in range(1, SHORT_CONV):
            acc = acc + w_ref[i:i + 1, :] * buf[t + i]
        o_ref[t] = _head_normalize(_silu(acc), sec)


def _prep_short(x_tm, state_tm, w_conv):
    steps, n, _ = x_tm.shape
    kern = functools.partial(_prep_short_kernel, steps=steps)
    return pl.pallas_call(
        kern,
        grid=(3,),
        in_specs=[
            pl.BlockSpec((steps, n, DN_WIDTH), lambda s: (0, 0, s)),
            pl.BlockSpec((SHORT_CONV - 1, n, DN_WIDTH), lambda s: (0, 0, s)),
            pl.BlockSpec((SHORT_CONV, DN_WIDTH), lambda s: (0, s)),
        ],
        out_specs=pl.BlockSpec((steps, n, DN_WIDTH), lambda s: (0, 0, s)),
        out_shape=jax.ShapeDtypeStruct((steps, n, 3 * DN_WIDTH), F32),
        compiler_params=_cp("arbitrary"),
        name="prep_sample",
    )(x_tm, state_tm, w_conv)


def _hdot(a, b):
    return jnp.dot(a, b, preferred_element_type=F32, precision=HIGHEST)


def _unit_lower_inverse(low, c):
    base = min(c, 16)
    row = lax.broadcasted_iota(jnp.int32, (c, c), 0)
    col = lax.broadcasted_iota(jnp.int32, (c, c), 1)
    eye = (row == col).astype(F32)
    diag = jnp.where((row // base) == (col // base), low, 0.0) if c > base else low
    inv = eye - diag
    power = diag
    k = 2
    while k < base:
        power = _hdot(power, power)
        inv = inv + _hdot(inv, power)
        k *= 2
    blk = base
    while blk < c:
        sel = ((row // (2 * blk)) == (col // (2 * blk))) & (((row // blk) % 2) == 1) & (((col // blk) % 2) == 0)
        off = jnp.where(sel, low, 0.0)
        inv = inv - _hdot(inv, _hdot(off, inv))
        blk *= 2
    return inv


def _delta_kernel(q_ref, k_ref, v_ref, gb_ref, z_ref, s0_ref, wn_ref, o_ref, sout_ref, s_scr, *, c):
    ci = pl.program_id(1)

    @pl.when(ci == 0)
    def _():
        s_scr[...] = s0_ref[0]

    row = lax.broadcasted_iota(jnp.int32, (c, c), 0)
    col = lax.broadcasted_iota(jnp.int32, (c, c), 1)
    causal = row >= col
    strict = row > col
    gb = gb_ref[...]
    gcum = _hdot(causal.astype(F32), gb)
    gcum_t = gcum.T
    wn = wn_ref[...]
    for h in range(N_HEADS):
        sl = slice(h * HEAD_DIM, (h + 1) * HEAD_DIM)
        q = q_ref[:, sl]
        k = k_ref[:, sl]
        v = v_ref[:, sl]
        g_col = gcum[:, h:h + 1]
        g_row = gcum_t[h:h + 1, :]
        beta = gb[:, N_HEADS + h:N_HEADS + h + 1]
        g_last = gcum[c - 1:c, h:h + 1]
        decay = jnp.where(causal, jnp.exp(g_col - g_row), 0.0)
        exp_g = jnp.exp(g_col)
        kb = k * beta
        kk = lax.dot_general(jnp.concatenate([kb, q], axis=0).astype(BF16), k.astype(BF16),
                             (((1,), (1,)), ((), ())), preferred_element_type=F32)
        lower = jnp.where(strict, kk[:c] * decay, 0.0)
        attn = kk[c:] * decay
        tinv = _unit_lower_inverse(lower, c)
        rhs = jnp.concatenate([v * beta, kb * exp_g], axis=-1)
        sol = _hdot(tinv, rhs)
        s = s_scr[h]
        both = _bdot(jnp.concatenate([sol[:, HEAD_DIM:], q * exp_g], axis=0), s)
        u = sol[:, :HEAD_DIM] - both[:c]
        o = both[c:] + _bdot(attn, u)
        kd = k * jnp.exp(g_last - g_col)
        s_new = s * jnp.exp(g_last) + lax.dot_general(kd.astype(BF16), u.astype(BF16), (((0,), (0,)), ((), ())),
                                                      preferred_element_type=F32)
        s_scr[h] = s_new
        on = o * lax.rsqrt(jnp.mean(o * o, axis=-1, keepdims=True) + EPS) * wn
        o_ref[:, sl] = (on * _silu(z_ref[:, sl])).astype(o_ref.dtype)

    @pl.when(ci == pl.num_programs(1) - 1)
    def _():
        sout_ref[0] = s_scr[...]


def _delta(qkv, gb, zsrc, z_col_block, s0, w_onorm, n_seq, n_chunks, c, row_block, aux_block, s0_index, name):
    kern = functools.partial(_delta_kernel, c=c)
    rows_out = n_seq * n_chunks * c
    return pl.pallas_call(
        kern,
        grid=(n_seq, n_chunks),
        in_specs=[
            pl.BlockSpec((c, DN_WIDTH), lambda n, ci: (row_block(n, ci), 0)),
            pl.BlockSpec((c, DN_WIDTH), lambda n, ci: (row_block(n, ci), 1)),
            pl.BlockSpec((c, DN_WIDTH), lambda n, ci: (row_block(n, ci), 2)),
            pl.BlockSpec((c, LANES), lambda n, ci: (aux_block(n, ci), 0)),
            pl.BlockSpec((c, DN_WIDTH), lambda n, ci: (aux_block(n, ci), z_col_block)),
            pl.BlockSpec((1, N_HEADS, HEAD_DIM, HEAD_DIM), lambda n, ci: (s0_index(n), 0, 0, 0)),
            pl.BlockSpec((1, HEAD_DIM), lambda n, ci: (0, 0)),
        ],
        out_specs=[
            pl.BlockSpec((c, DN_WIDTH), lambda n, ci: (n * n_chunks + ci, 0)),
            pl.BlockSpec((1, N_HEADS, HEAD_DIM, HEAD_DIM), lambda n, ci: (n, 0, 0, 0)),
        ],
        out_shape=[
            jax.ShapeDtypeStruct((rows_out, DN_WIDTH), BF16),
            jax.ShapeDtypeStruct((n_seq, N_HEADS, HEAD_DIM, HEAD_DIM), F32),
        ],
        scratch_shapes=[pltpu.VMEM((N_HEADS, HEAD_DIM, HEAD_DIM), F32)],
        compiler_params=_cp("parallel", "arbitrary"),
        name=name,
    )(qkv, qkv, qkv, gb, zsrc, s0, w_onorm.reshape(1, HEAD_DIM))


def _layernorm_silu(c, lnw, lnb):
    mu = jnp.mean(c, axis=-1, keepdims=True)
    xc = c - mu
    y = xc * lax.rsqrt(jnp.mean(xc * xc, axis=-1, keepdims=True) + EPS)
    return _silu(y * lnw + lnb)


def _conf_long_kernel(cur_ref, prev_ref, init_ref, w_ref, b_ref, lnw_ref, lnb_ref, o_ref, ext_ref, *, rows, blocks_per_seq):
    hist = CONF_KERNEL - 1
    first = (pl.program_id(0) % blocks_per_seq) == 0
    ext_ref[0:32, :] = jnp.where(first, init_ref[...], prev_ref[...])
    ext_ref[32:, :] = cur_ref[...]
    for r0 in range(0, rows, 8):
        acc = w_ref[hist:hist + 1, :] * cur_ref[r0:r0 + 8, :]
        for i in range(hist):
            acc = acc + w_ref[i:i + 1, :] * ext_ref[pl.ds(32 - hist + r0 + i, 8), :]
        o_ref[r0:r0 + 8, :] = _layernorm_silu(acc + b_ref[...], lnw_ref[...], lnb_ref[...]).astype(o_ref.dtype)


def _conf_long(glu, init_hist, w_dw, b_dw, ln_w, ln_b, n_rows, rows, row_block0, blocks_per_seq, prev_index, name):
    cw = glu.shape[1]
    nblk = n_rows // rows
    kern = functools.partial(_conf_long_kernel, rows=rows, blocks_per_seq=blocks_per_seq)
    vec = lambda a: a.reshape(1, cw)
    return pl.pallas_call(
        kern,
        grid=(nblk,),
        in_specs=[
            pl.BlockSpec((rows, cw), lambda i: (i + row_block0, 0)),
            pl.BlockSpec((32, cw), lambda i: (prev_index(i), 0)),
            pl.BlockSpec((32, cw), lambda i: (0, 0)),
            pl.BlockSpec((32, cw), lambda i: (0, 0)),
            pl.BlockSpec((1, cw), lambda i: (0, 0)),
            pl.BlockSpec((1, cw), lambda i: (0, 0)),
            pl.BlockSpec((1, cw), lambda i: (0, 0)),
        ],
        out_specs=pl.BlockSpec((rows, cw), lambda i: (i, 0)),
        out_shape=jax.ShapeDtypeStruct((n_rows, cw), BF16),
        scratch_shapes=[pltpu.VMEM((rows + 32, cw), F32)],
        compiler_params=_cp("parallel"),
        name=name,
    )(glu, glu, init_hist, jnp.pad(w_dw, ((0, 1), (0, 0))), vec(b_dw), vec(ln_w), vec(ln_b))


def _conf_short_kernel(x_ref, hist_ref, w_ref, b_ref, lnw_ref, lnb_ref, o_ref, *, steps):
    nh = CONF_KERNEL - 1
    for t in range(steps):
        acc = jnp.zeros(x_ref.shape[1:], F32)
        for i in range(CONF_KERNEL):
            j = t + i
            src = hist_ref[j] if j < nh else x_ref[j - nh]
            acc = acc + w_ref[i:i + 1, :] * src
        o_ref[t] = _layernorm_silu(acc + b_ref[...], lnw_ref[...], lnb_ref[...]).astype(o_ref.dtype)


def _conf_short(x_tm, hist_tm, w_dw, b_dw, ln_w, ln_b, nb=32):
    steps, n, cw = x_tm.shape
    kern = functools.partial(_conf_short_kernel, steps=steps)
    vec = lambda a: a.reshape(1, cw)
    return pl.pallas_call(
        kern,
        grid=(n // nb,),
        in_specs=[
            pl.BlockSpec((steps, nb, cw), lambda i: (0, i, 0)),
            pl.BlockSpec((CONF_KERNEL - 1, nb, cw), lambda i: (0, i, 0)),
            pl.BlockSpec((CONF_KERNEL, cw), lambda i: (0, 0)),
            pl.BlockSpec((1, cw), lambda i: (0, 0)),
            pl.BlockSpec((1, cw), lambda i: (0, 0)),
            pl.BlockSpec((1, cw), lambda i: (0, 0)),
        ],
        out_specs=pl.BlockSpec((steps, nb, cw), lambda i: (0, i, 0)),
        out_shape=jax.ShapeDtypeStruct((steps, n, cw), BF16),
        compiler_params=_cp("parallel"),
        name="conf_sample",
    )(x_tm, hist_tm, w_dw, vec(b_dw), vec(ln_w), vec(ln_b))


def _merge_kernel(oz_ref, cz_ref, ga_ref, gb_ref, wa_ref, wb_ref, bias_ref, o_ref):
    ya = jnp.dot(oz_ref[...], wa_ref[...], preferred_element_type=F32)
    yb = jnp.dot(cz_ref[...], wb_ref[...], preferred_element_type=F32) + bias_ref[...]
    o_ref[...] = (ga_ref[...].astype(F32) * ya + gb_ref[...].astype(F32) * yb).astype(o_ref.dtype)


def _merge(oz, cz, gates, wa, wb, bias):
    t, ka = oz.shape
    kb = cz.shape[1]
    d = wa.shape[1]
    nb = d // TN
    return pl.pallas_call(
        _merge_kernel,
        grid=(pl.cdiv(t, TM), nb),
        in_specs=[
            pl.BlockSpec((TM, ka), lambda i, j: (i, 0)),
            pl.BlockSpec((TM, kb), lambda i, j: (i, 0)),
            pl.BlockSpec((TM, TN), lambda i, j: (i, j)),
            pl.BlockSpec((TM, TN), lambda i, j: (i, j + nb)),
            pl.BlockSpec((ka, TN), lambda i, j: (0, j)),
            pl.BlockSpec((kb, TN), lambda i, j: (0, j)),
            pl.BlockSpec((1, TN), lambda i, j: (0, j)),
        ],
        out_specs=pl.BlockSpec((TM, TN), lambda i, j: (i, j)),
        out_shape=jax.ShapeDtypeStruct((t, d), BF16),
        compiler_params=_cp("parallel", "arbitrary"),
        name="merge",
    )(oz, cz, gates, gates, wa, wb, bias.reshape(1, d))


def _out_kernel(m_ref, w_ref, x_ref, o_ref):
    o_ref[...] = x_ref[...] + jnp.dot(m_ref[...], w_ref[...], preferred_element_type=F32)


def _out_proj(merged, w_out, x):
    t, k = merged.shape
    d = w_out.shape[1]
    return pl.pallas_call(
        _out_kernel,
        grid=(pl.cdiv(t, TM), d // TN),
        in_specs=[
            pl.BlockSpec((TM, k), lambda i, j: (i, 0)),
            pl.BlockSpec((k, TN), lambda i, j: (0, j)),
            pl.BlockSpec((TM, TN), lambda i, j: (i, j)),
        ],
        out_specs=pl.BlockSpec((TM, TN), lambda i, j: (i, j)),
        out_shape=jax.ShapeDtypeStruct((t, d), F32),
        compiler_params=_cp("parallel", "arbitrary"),
        name="out_proj",
    )(merged, w_out, x)


def _router_kernel(h_ref, nw_ref, wr_ref, br_ref, xn_ref, ids_ref, wts_ref):
    x = h_ref[...]
    xn = x * lax.rsqrt(jnp.mean(x * x, axis=-1, keepdims=True) + EPS) * nw_ref[...]
    xn_ref[...] = xn.astype(xn_ref.dtype)
    logits = _hdot(xn, wr_ref[...]) + br_ref[...]
    lane = lax.broadcasted_iota(jnp.int32, logits.shape, 1)
    neg = -jnp.inf
    big = jnp.int32(1 << 20)
    gl = jnp.where(lane < N_GROUPS, logits, neg)
    gmax = jnp.max(gl, axis=-1, keepdims=True)
    gsum = jnp.sum(jnp.exp(gl - gmax), axis=-1, keepdims=True)
    pg_top = 1.0 / gsum
    gidx = jnp.min(jnp.where(gl == gmax, lane, big), axis=-1, keepdims=True)
    lo = N_GROUPS + gidx * EXPERTS_PER_GROUP
    in_grp = (lane >= lo) & (lane < lo + EXPERTS_PER_GROUP)
    el = jnp.where(in_grp, logits, neg)
    emax = jnp.max(el, axis=-1, keepdims=True)
    ex = jnp.exp(el - emax)
    esum = jnp.sum(ex, axis=-1, keepdims=True)
    pe = ex / esum
    pe = jnp.where(in_grp, pe, -1.0)
    p1 = jnp.max(pe, axis=-1, keepdims=True)
    i1 = jnp.min(jnp.where(pe == p1, lane, big), axis=-1, keepdims=True)
    pe2 = jnp.where(lane == i1, -1.0, pe)
    p2 = jnp.max(pe2, axis=-1, keepdims=True)
    i2 = jnp.min(jnp.where(pe2 == p2, lane, big), axis=-1, keepdims=True)
    denom = p1 + p2
    w1 = pg_top * p1 / denom
    w2 = pg_top * p2 / denom
    ids_ref[...] = jnp.where(lane == 0, i1 - N_GROUPS, jnp.where(lane == 1, i2 - N_GROUPS, 0))
    wts_ref[...] = jnp.where(lane == 0, w1, jnp.where(lane == 1, w2, 0.0))


def _router(h1, norm_w, w_r, b_r, rows=256):
    t, d = h1.shape
    return pl.pallas_call(
        _router_kernel,
        grid=(pl.cdiv(t, rows),),
        in_specs=[
            pl.BlockSpec((rows, d), lambda i: (i, 0)),
            pl.BlockSpec((1, d), lambda i: (0, 0)),
            pl.BlockSpec((d, LANES), lambda i: (0, 0)),
            pl.BlockSpec((1, LANES), lambda i: (0, 0)),
        ],
        out_specs=[
            pl.BlockSpec((rows, d), lambda i: (i, 0)),
            pl.BlockSpec((rows, LANES), lambda i: (i, 0)),
            pl.BlockSpec((rows, LANES), lambda i: (i, 0)),
        ],
        out_shape=[
            jax.ShapeDtypeStruct((t, d), BF16),
            jax.ShapeDtypeStruct((t, LANES), jnp.int32),
            jax.ShapeDtypeStruct((t, LANES), F32),
        ],
        compiler_params=_cp("parallel"),
        name="router",
    )(h1, norm_w.reshape(1, d), w_r, b_r)


def _experts_kernel(ve_ref, nv_ref, xs_ref, wg_ref, wu_ref, wd_ref, o_ref, acc_ref):
    v = pl.program_id(0)
    f = pl.program_id(1)

    @pl.when(v < nv_ref[0])
    def _():
        x = xs_ref[...]
        g = jnp.dot(x, wg_ref[0].astype(BF16), preferred_element_type=F32)
        u = jnp.dot(x, wu_ref[0].astype(BF16), preferred_element_type=F32)
        hmid = (_silu(g) * u).astype(BF16)
        part = jnp.dot(hmid, wd_ref[0].astype(BF16), preferred_element_type=F32)

        @pl.when(f == 0)
        def _():
            acc_ref[...] = part

        @pl.when(f > 0)
        def _():
            acc_ref[...] += part

        @pl.when(f == pl.num_programs(1) - 1)
        def _():
            o_ref[...] = acc_ref[...].astype(o_ref.dtype)


def _experts(visit_expert, n_visits, xs, w_gate, w_up, w_down, max_visits):
    d = xs.shape[1]
    fdim = w_gate.shape[2]
    nf = fdim // EXPERT_F_TILE
    last = lambda v, nv: jnp.minimum(v, nv[0] - 1)
    grid_spec = pltpu.PrefetchScalarGridSpec(
        num_scalar_prefetch=2,
        grid=(max_visits, nf),
        in_specs=[
            pl.BlockSpec((GROUP_ROWS, d), lambda v, f, ve, nv: (last(v, nv), 0)),
            pl.BlockSpec((1, d, EXPERT_F_TILE), lambda v, f, ve, nv: (ve[v], 0, jnp.where(v < nv[0], f, nf - 1))),
            pl.BlockSpec((1, d, EXPERT_F_TILE), lambda v, f, ve, nv: (ve[v], 0, jnp.where(v < nv[0], f, nf - 1))),
            pl.BlockSpec((1, EXPERT_F_TILE, d), lambda v, f, ve, nv: (ve[v], jnp.where(v < nv[0], f, nf - 1), 0)),
        ],
        out_specs=pl.BlockSpec((GROUP_ROWS, d), lambda v, f, ve, nv: (last(v, nv), 0)),
        scratch_shapes=[pltpu.VMEM((GROUP_ROWS, d), F32)],
    )
    return pl.pallas_call(
        _experts_kernel,
        grid_spec=grid_spec,
        out_shape=jax.ShapeDtypeStruct((max_visits * GROUP_ROWS, d), F32),
        compiler_params=_cp("arbitrary", "arbitrary"),
        name="experts",
    )(visit_expert, n_visits, xs, w_gate, w_up, w_down)


def _final_kernel(h_ref, o1_ref, o2_ref, wts_ref, nw_ref, y_ref):
    wts = wts_ref[...]
    h = h_ref[...] + wts[:, 0:1] * o1_ref[...] + wts[:, 1:2] * o2_ref[...]
    y_ref[...] = h * lax.rsqrt(jnp.mean(h * h, axis=-1, keepdims=True) + EPS) * nw_ref[...]


FINAL_ROWS = 256


def _final(h1, o1, o2, wts, norm_w, n_rows, row_block0, rows=FINAL_ROWS):
    d = h1.shape[1]
    rb = lambda i: (i + row_block0, 0)
    return pl.pallas_call(
        _final_kernel,
        grid=(n_rows // rows,),
        in_specs=[
            pl.BlockSpec((rows, d), rb),
            pl.BlockSpec((rows, d), rb),
            pl.BlockSpec((rows, d), rb),
            pl.BlockSpec((rows, LANES), rb),
            pl.BlockSpec((1, d), lambda i: (0, 0)),
        ],
        out_specs=pl.BlockSpec((rows, d), lambda i: (i, 0)),
        out_shape=jax.ShapeDtypeStruct((n_rows, d), F32),
        compiler_params=_cp("parallel"),
        name="final",
    )(h1, o1, o2, wts, norm_w.reshape(1, d))


def _dispatch_plan(ids, n_tokens, max_visits):
    flat_e = ids[:, :2].reshape(-1)
    n_slots = flat_e.shape[0]
    flat_tok = jnp.arange(n_slots, dtype=jnp.int32) // 2
    order = jnp.argsort(flat_e, stable=True)
    sorted_e = flat_e[order]
    counts = jnp.zeros((N_EXPERTS,), jnp.int32).at[flat_e].add(1)
    tiles = (counts + GROUP_ROWS - 1) // GROUP_ROWS
    tile_end = jnp.cumsum(tiles)
    tile_start = tile_end - tiles
    group_start = jnp.cumsum(counts) - counts
    rank = jnp.arange(n_slots, dtype=jnp.int32) - group_start[sorted_e]
    padded_pos = tile_start[sorted_e] * GROUP_ROWS + rank
    n_visits = tile_end[-1]
    row_token = jnp.zeros((max_visits * GROUP_ROWS,), jnp.int32).at[padded_pos].set(flat_tok[order])
    slot_pos = jnp.zeros((n_slots,), jnp.int32).at[order].set(padded_pos).reshape(n_tokens, 2)
    visit = jnp.arange(max_visits, dtype=jnp.int32)
    visit_expert = jnp.searchsorted(tile_end, jnp.minimum(visit, n_visits - 1), side="right").astype(jnp.int32)
    visit_expert = jnp.minimum(visit_expert, N_EXPERTS - 1)
    return row_token, slot_pos, visit_expert, n_visits.reshape(1).astype(jnp.int32)


def kernel(x_prompt, x_sample, state_delta, state_conv_qkv, state_conv_b, meta_tokens,
           norm1, w_in, w_conv_qkv, a_log, dt_bias, w_onorm, w_proj_a, w_dw, b_dw, ln_w, ln_b,
           w_proj_b, b_proj_b, w_out, norm2, w_rg, b_rg, w_re, b_re, w_e_gate, w_e_up, w_e_down,
           final_norm):
    bsz, seq, d = x_prompt.shape
    nsmp, steps, _ = x_sample.shape
    n_meta = meta_tokens.shape[0]
    depth = norm1.shape[0]
    assert depth == 1
    tp = bsz * seq
    ts = nsmp * steps
    t_all = tp + ts + n_meta
    qkv_w = 3 * DN_WIDTH
    conf_w = w_dw.shape[-1]
    hist = CONF_KERNEL - 1

    x_all = jnp.concatenate([x_prompt.reshape(tp, d), x_sample.transpose(1, 0, 2).reshape(ts, d), meta_tokens], axis=0)

    wi = w_in[0]
    o_z = qkv_w + DN_WIDTH
    o_glu = o_z + 2 * N_HEADS
    o_gate = o_glu + 2 * conf_w
    w_qkvz = wi[:, :o_z].astype(BF16)
    w_ab = jnp.pad(wi[:, o_z:o_glu], ((0, 0), (0, LANES - 2 * N_HEADS))).astype(BF16)
    w_glu = wi[:, o_glu:o_gate].astype(BF16)
    w_gate = wi[:, o_gate:].astype(BF16)
    alog_pad = jnp.pad(a_log[0], (0, LANES - N_HEADS)).reshape(1, LANES)
    dtb_pad = jnp.pad(dt_bias[0], (0, LANES - N_HEADS)).reshape(1, LANES)

    hn = _rmsnorm_bf16(x_all, norm1[0])
    qkvz = _proj(_mm_plain_kernel, hn, w_qkvz, o_z, F32, "proj_qkvz")
    gbeta = _proj_decay(hn, w_ab, alog_pad, dtb_pad)
    glu = _proj_glu(hn, w_glu)
    gates = _proj(_mm_sigmoid_kernel, hn, w_gate, 2 * d, BF16, "proj_gates")

    wc = w_conv_qkv[0]
    meta_blk = (tp + ts) // n_meta
    qkv_meta = _prep_long(qkvz, wc, n_meta, n_meta, meta_blk, lambda i: 0, True, "prep_meta")
    rows_p = 256
    bps = seq // rows_p
    meta_halo = (tp + ts + n_meta) // 8 - 1
    qkv_p = _prep_long(qkvz, wc, tp, rows_p, 0,
                       lambda i: jnp.where(i % bps == 0, meta_halo, i * (rows_p // 8) - 1), False, "prep_prompt")
    raw_s = qkvz[tp:tp + ts, :qkv_w].reshape(steps, nsmp, qkv_w)
    st_qkv_tm = state_conv_qkv[0].transpose(1, 0, 2)
    qkv_s_tm = _prep_short(raw_s, st_qkv_tm, wc)

    zero_state = jnp.zeros((1, N_HEADS, HEAD_DIM, HEAD_DIM), F32)
    oz_meta, s_meta = _delta(qkv_meta, gbeta, qkvz, 3, zero_state, w_onorm[0], 1, 1, n_meta,
                             lambda n, ci: 0, lambda n, ci: meta_blk, lambda n: 0, "delta_meta")
    n_chunks = seq // CHUNK
    prompt_blk = lambda n, ci: n * n_chunks + ci
    oz_p, s_p = _delta(qkv_p, gbeta, qkvz, 3, s_meta, w_onorm[0], bsz, n_chunks, CHUNK,
                       prompt_blk, prompt_blk, lambda n: 0, "delta_prompt")

    cpad = 8
    to_bm = lambda a: jnp.pad(a.transpose(1, 0, 2), ((0, 0), (0, cpad - steps), (0, 0))).reshape(nsmp * cpad, a.shape[-1])
    qkv_s = to_bm(qkv_s_tm)
    gb_s = to_bm(gbeta[tp:tp + ts].reshape(steps, nsmp, LANES))
    z_s = to_bm(qkvz[tp:tp + ts, qkv_w:].reshape(steps, nsmp, DN_WIDTH))
    oz_s_bm, s_s = _delta(qkv_s, gb_s, z_s, 0, state_delta[0], w_onorm[0], nsmp, 1, cpad,
                          lambda n, ci: n, lambda n, ci: n, lambda n: n, "delta_sample")
    oz_s = oz_s_bm.reshape(nsmp, cpad, DN_WIDTH)[:, :steps].transpose(1, 0, 2).reshape(ts, DN_WIDTH)
    oz = jnp.concatenate([oz_p, oz_s, oz_meta], axis=0)

    zeros32 = jnp.zeros((32, conf_w), F32)
    cz_meta = _conf_long(glu, zeros32, w_dw[0], b_dw[0], ln_w[0], ln_b[0], n_meta, n_meta, meta_blk, 1,
                         lambda i: 0, "conf_meta")
    init_p = jnp.concatenate([jnp.zeros((32 - n_meta, conf_w), F32), glu[tp + ts:]], axis=0)
    rows_c = 128
    cz_p = _conf_long(glu, init_p, w_dw[0], b_dw[0], ln_w[0], ln_b[0], tp, rows_c, 0, seq // rows_c,
                      lambda i: jnp.maximum(i * (rows_c // 32) - 1, 0), "conf_prompt")
    glu_s_tm = glu[tp:tp + ts].reshape(steps, nsmp, conf_w)
    hist_s_tm = state_conv_b[0].transpose(1, 0, 2)
    cz_s = _conf_short(glu_s_tm, hist_s_tm, w_dw[0], b_dw[0], ln_w[0], ln_b[0]).reshape(ts, conf_w)
    cz = jnp.concatenate([cz_p, cz_s, cz_meta], axis=0)

    merged = _merge(oz, cz, gates, w_proj_a[0].astype(BF16), w_proj_b[0].astype(BF16), b_proj_b[0])
    h1 = _out_proj(merged, w_out[0].astype(BF16), x_all)

    w_r = jnp.pad(jnp.concatenate([w_rg[0], w_re[0]], axis=1), ((0, 0), (0, LANES - N_GROUPS - N_EXPERTS)))
    b_r = jnp.pad(jnp.concatenate([b_rg[0], b_re[0]]), (0, LANES - N_GROUPS - N_EXPERTS)).reshape(1, LANES)
    xn2, ids, wts = _router(h1, norm2[0], w_r, b_r)
    max_visits = N_EXPERTS + (2 * t_all) // GROUP_ROWS
    row_token, slot_pos, visit_expert, n_visits = _dispatch_plan(ids, t_all, max_visits)
    xs = xn2[row_token]
    eo = _experts(visit_expert, n_visits, xs, w_e_gate[0], w_e_up[0], w_e_down[0], max_visits)
    o1 = eo[slot_pos[:, 0]]
    o2 = eo[slot_pos[:, 1]]
    y_p = _final(h1, o1, o2, wts, final_norm, tp, 0)
    y_s = _final(h1, o1, o2, wts, final_norm, ts, tp // FINAL_ROWS)

    y_prompt = y_p.reshape(bsz, seq, d)
    y_sample = y_s.reshape(steps, nsmp, d).transpose(1, 0, 2)
    new_cq_p = qkvz[:tp, :qkv_w].reshape(bsz, seq, qkv_w)[:, seq - (SHORT_CONV - 1):]
    new_cb_p = glu[:tp].reshape(bsz, seq, conf_w)[:, seq - hist:]
    cq_s = jnp.concatenate([st_qkv_tm, raw_s], axis=0)[-(SHORT_CONV - 1):].transpose(1, 0, 2)
    cb_s = jnp.concatenate([hist_s_tm, glu_s_tm], axis=0)[-hist:].transpose(1, 0, 2)
    return (y_prompt, y_sample, s_p[None], new_cq_p[None], new_cb_p[None], s_s[None], cq_s[None], cb_s[None])
```

```python
import functools

import jax
import jax.numpy as jnp
from jax import lax
from jax.experimental import pallas as pl
from jax.experimental.pallas import tpu as pltpu

F32 = jnp.float32
BF16 = jnp.bfloat16
EPS = 1e-6

N_HEADS = 16
HEAD_DIM = 128
DN_WIDTH = N_HEADS * HEAD_DIM
SHORT_CONV = 4
CONF_KERNEL = 31
N_GROUPS = 8
EXPERTS_PER_GROUP = 8
N_EXPERTS = N_GROUPS * EXPERTS_PER_GROUP
CHUNK = 64
LANES = 128
HIGHEST = lax.Precision.HIGHEST

TM = 1024
TN = 512
GROUP_ROWS = 320
EXPERT_F_TILE = 256
CONF_LANE_CHUNK = 512
FINAL_ROWS = 256
VMEM_LIMIT = 56 * 1024 * 1024


def _cp(*sem):
    return pltpu.CompilerParams(dimension_semantics=sem, vmem_limit_bytes=VMEM_LIMIT)


def _sigmoid(x):
    return 1.0 / (1.0 + jnp.exp(-x))


def _silu(x):
    return x * _sigmoid(x)


def _softplus(x):
    return jnp.maximum(x, 0.0) + jnp.log1p(jnp.exp(-jnp.abs(x)))


def _rmsnorm_kernel(x_ref, w_ref, o_ref):
    x = x_ref[...]
    ms = jnp.mean(x * x, axis=-1, keepdims=True)
    o_ref[...] = (x * lax.rsqrt(ms + EPS) * w_ref[...]).astype(o_ref.dtype)


def _rmsnorm_bf16(x, w, rows=256):
    t, d = x.shape
    return pl.pallas_call(
        _rmsnorm_kernel,
        grid=(pl.cdiv(t, rows),),
        in_specs=[pl.BlockSpec((rows, d), lambda i: (i, 0)), pl.BlockSpec((1, d), lambda i: (0, 0))],
        out_specs=pl.BlockSpec((rows, d), lambda i: (i, 0)),
        out_shape=jax.ShapeDtypeStruct((t, d), BF16),
        compiler_params=_cp("parallel"),
        name="rmsnorm1",
    )(x, w.reshape(1, d))


def _mm_plain_kernel(x_ref, w_ref, o_ref):
    o_ref[...] = jnp.dot(x_ref[...], w_ref[...], preferred_element_type=F32).astype(o_ref.dtype)


def _mm_sigmoid_kernel(x_ref, w_ref, o_ref):
    o_ref[...] = _sigmoid(jnp.dot(x_ref[...], w_ref[...], preferred_element_type=F32)).astype(o_ref.dtype)


def _mm_glu_kernel(x_ref, wu_ref, wg_ref, o_ref):
    x = x_ref[...]
    u = jnp.dot(x, wu_ref[...], preferred_element_type=F32)
    g = jnp.dot(x, wg_ref[...], preferred_element_type=F32)
    o_ref[...] = u * _sigmoid(g)


def _mm_decay_kernel(x_ref, w_ref, alog_ref, dtb_ref, o_ref):
    acc = jnp.dot(x_ref[...], w_ref[...], preferred_element_type=F32)
    lane = lax.broadcasted_iota(jnp.int32, acc.shape, 1)
    g = -jnp.exp(alog_ref[...]) * _softplus(acc + dtb_ref[...])
    beta = _sigmoid(acc)
    o_ref[...] = jnp.where(lane < N_HEADS, g, jnp.where(lane < 2 * N_HEADS, beta, 0.0))


def _proj(kernel_fn, hn, w, n_out, out_dtype, name, tn=TN):
    t, k = hn.shape
    return pl.pallas_call(
        kernel_fn,
        grid=(pl.cdiv(t, TM), n_out // tn),
        in_specs=[pl.BlockSpec((TM, k), lambda i, j: (i, 0)), pl.BlockSpec((k, tn), lambda i, j: (0, j))],
        out_specs=pl.BlockSpec((TM, tn), lambda i, j: (i, j)),
        out_shape=jax.ShapeDtypeStruct((t, n_out), out_dtype),
        compiler_params=_cp("parallel", "arbitrary"),
        name=name,
    )(hn, w)


def _proj_glu(hn, w_glu):
    t, k = hn.shape
    half = w_glu.shape[1] // 2
    nb = half // TN
    return pl.pallas_call(
        _mm_glu_kernel,
        grid=(pl.cdiv(t, TM), nb),
        in_specs=[
            pl.BlockSpec((TM, k), lambda i, j: (i, 0)),
            pl.BlockSpec((k, TN), lambda i, j: (0, j)),
            pl.BlockSpec((k, TN), lambda i, j: (0, j + nb)),
        ],
        out_specs=pl.BlockSpec((TM, TN), lambda i, j: (i, j)),
        out_shape=jax.ShapeDtypeStruct((t, half), F32),
        compiler_params=_cp("parallel", "arbitrary"),
        name="proj_glu",
    )(hn, w_glu, w_glu)


def _proj_decay(hn, w_ab, alog_pad, dtb_pad):
    t, k = hn.shape
    return pl.pallas_call(
        _mm_decay_kernel,
        grid=(pl.cdiv(t, TM),),
        in_specs=[
            pl.BlockSpec((TM, k), lambda i: (i, 0)),
            pl.BlockSpec((k, LANES), lambda i: (0, 0)),
            pl.BlockSpec((1, LANES), lambda i: (0, 0)),
            pl.BlockSpec((1, LANES), lambda i: (0, 0)),
        ],
        out_specs=pl.BlockSpec((TM, LANES), lambda i: (i, 0)),
        out_shape=jax.ShapeDtypeStruct((t, LANES), F32),
        compiler_params=_cp("parallel"),
        name="proj_decay",
    )(hn, w_ab, alog_pad, dtb_pad)


def _head_normalize(y, sec):
    outs = []
    for h in range(N_HEADS):
        yh = y[:, h * HEAD_DIM:(h + 1) * HEAD_DIM]
        inv = lax.rsqrt(jnp.sum(yh * yh, axis=-1, keepdims=True) + EPS)
        scale = jnp.where(sec == 0, inv * (HEAD_DIM ** -0.5), jnp.where(sec == 1, inv, 1.0))
        outs.append(yh * scale)
    return jnp.concatenate(outs, axis=-1)


def _prep_long_kernel(cur_ref, halo_ref, w_ref, o_ref, ext_ref, *, rows, zero_halo):
    sec = pl.program_id(1)
    halo = halo_ref[...]
    ext_ref[0:8, :] = jnp.zeros_like(halo) if zero_halo else halo
    ext_ref[8:, :] = cur_ref[...]
    acc = w_ref[SHORT_CONV - 1:SHORT_CONV, :] * cur_ref[...]
    for s in range(1, SHORT_CONV):
        acc = acc + w_ref[SHORT_CONV - 1 - s:SHORT_CONV - s, :] * ext_ref[pl.ds(8 - s, rows), :]
    o_ref[...] = _head_normalize(_silu(acc), sec)


def _prep_long(qkvz, w_conv, n_rows, rows, row_block0, halo_index, zero_halo, name):
    nblk = n_rows // rows
    kern = functools.partial(_prep_long_kernel, rows=rows, zero_halo=zero_halo)
    return pl.pallas_call(
        kern,
        grid=(nblk, 3),
        in_specs=[
            pl.BlockSpec((rows, DN_WIDTH), lambda i, s: (i + row_block0, s)),
            pl.BlockSpec((8, DN_WIDTH), lambda i, s: (halo_index(i), s)),
            pl.BlockSpec((SHORT_CONV, DN_WIDTH), lambda i, s: (0, s)),
        ],
        out_specs=pl.BlockSpec((rows, DN_WIDTH), lambda i, s: (i, s)),
        out_shape=jax.ShapeDtypeStruct((n_rows, 3 * DN_WIDTH), F32),
        scratch_shapes=[pltpu.VMEM((rows + 8, DN_WIDTH), F32)],
        compiler_params=_cp("parallel", "arbitrary"),
        name=name,
    )(qkvz, qkvz, w_conv)


def _prep_short_kernel(x_ref, st_ref, w_ref, o_ref, *, steps):
    sec = pl.program_id(0)
    buf = [st_ref[i] for i in range(SHORT_CONV - 1)] + [x_ref[t] for t in range(steps)]
    for t in range(steps):
        acc = w_ref[0:1, :] * buf[t]
        for i in range(1, SHORT_CONV):
            acc = acc + w_ref[i:i + 1, :] * buf[t + i]
        o_ref[t] = _head_normalize(_silu(acc), sec)


def _prep_short(x_tm, state_tm, w_conv):
    steps, n, _ = x_tm.shape
    kern = functools.partial(_prep_short_kernel, steps=steps)
    return pl.pallas_call(
        kern,
        grid=(3,),
        in_specs=[
            pl.BlockSpec((steps, n, DN_WIDTH), lambda s: (0, 0, s)),
            pl.BlockSpec((SHORT_CONV - 1, n, DN_WIDTH), lambda s: (0, 0, s)),
            pl.BlockSpec((SHORT_CONV, DN_WIDTH), lambda s: (0, s)),
        ],
        out_specs=pl.BlockSpec((steps, n, DN_WIDTH), lambda s: (0, 0, s)),
        out_shape=jax.ShapeDtypeStruct((steps, n, 3 * DN_WIDTH), F32),
        compiler_params=_cp("arbitrary"),
        name="prep_sample",
    )(x_tm, state_tm, w_conv)


def _hdot(a, b):
    return jnp.dot(a, b, preferred_element_type=F32, precision=HIGHEST)


def _split_bf16(a):
    hi = a.astype(BF16)
    lo = (a - hi.astype(F32)).astype(BF16)
    return hi, lo


def _bmm(a, b):
    return jnp.einsum("hik,hkj->hij", a.astype(BF16), b.astype(BF16), preferred_element_type=F32)


def _bmm3(a, b):
    ah, al = _split_bf16(a)
    bh, bl = _split_bf16(b)
    f = lambda x, y: jnp.einsum("hik,hkj->hij", x, y, preferred_element_type=F32)
    return f(ah, bh) + (f(ah, bl) + f(al, bh))


def _unit_lower_inverse(low, c):
    base = min(c, 16)
    row = lax.broadcasted_iota(jnp.int32, (c, c), 0)
    col = lax.broadcasted_iota(jnp.int32, (c, c), 1)
    eye = (row == col).astype(F32)[None]
    diag = jnp.where(((row // base) == (col // base))[None], low, 0.0) if c > base else low
    inv = eye - diag
    power = diag
    k = 2
    while k < base:
        power = _bmm3(power, power)
        inv = inv + _bmm3(inv, power)
        k *= 2
    blk = base
    while blk < c:
        sel = ((row // (2 * blk)) == (col // (2 * blk))) & (((row // blk) % 2) == 1) & (((col // blk) % 2) == 0)
        off = jnp.where(sel[None], low, 0.0)
        inv = inv - _bmm3(inv, _bmm3(off, inv))
        blk *= 2
    return inv


def _delta_kernel(q_ref, k_ref, v_ref, gb_ref, z_ref, s0_ref, wn_ref, o_ref, sout_ref, s_scr, *, c, heads_per_group):
    ci = pl.program_id(1)

    @pl.when(ci == 0)
    def _():
        s_scr[...] = s0_ref[0]

    row = lax.broadcasted_iota(jnp.int32, (c, c), 0)
    col = lax.broadcasted_iota(jnp.int32, (c, c), 1)
    causal = (row >= col)[None]
    strict = (row > col)[None]
    gb = gb_ref[...]
    gcum = _hdot((row >= col).astype(F32), gb)
    gcum_t = gcum.T
    wn = wn_ref[...]
    d = HEAD_DIM
    for h0 in range(0, N_HEADS, heads_per_group):
        heads = range(h0, h0 + heads_per_group)
        hs = slice(h0, h0 + heads_per_group)
        heads_of = lambda ref: jnp.stack([ref[:, h * d:(h + 1) * d] for h in heads])
        q = heads_of(q_ref)
        k = heads_of(k_ref)
        v = heads_of(v_ref)
        g_col = jnp.stack([gcum[:, h:h + 1] for h in heads])
        g_row = jnp.stack([gcum_t[h:h + 1, :] for h in heads])
        beta = jnp.stack([gb[:, N_HEADS + h:N_HEADS + h + 1] for h in heads])
        g_last = g_col[:, c - 1:c, :]
        decay = jnp.where(causal, jnp.exp(g_col - g_row), 0.0)
        exp_g = jnp.exp(g_col)
        kb = k * beta
        kk = jnp.einsum("hid,hjd->hij", jnp.concatenate([kb, q], axis=1).astype(BF16), k.astype(BF16),
                        preferred_element_type=F32)
        lower = jnp.where(strict, kk[:, :c] * decay, 0.0)
        attn = kk[:, c:] * decay
        tinv = _unit_lower_inverse(lower, c)
        rhs = jnp.concatenate([v * beta, kb * exp_g], axis=-1)
        sol = _bmm3(tinv, rhs)
        s = s_scr[hs]
        both = _bmm(jnp.concatenate([sol[:, :, d:], q * exp_g], axis=1), s)
        u = sol[:, :, :d] - both[:, :c]
        o = both[:, c:] + _bmm(attn, u)
        kd = k * jnp.exp(g_last - g_col)
        upd = lax.dot_general(kd.astype(BF16), u.astype(BF16), (((1,), (1,)), ((0,), (0,))),
                              preferred_element_type=F32)
        s_scr[hs] = s * jnp.exp(g_last) + upd
        on = o * lax.rsqrt(jnp.mean(o * o, axis=-1, keepdims=True) + EPS) * wn
        out = (on * _silu(heads_of(z_ref))).astype(o_ref.dtype)
        for i, h in enumerate(heads):
            o_ref[:, h * d:(h + 1) * d] = out[i]

    @pl.when(ci == pl.num_programs(1) - 1)
    def _():
        sout_ref[0] = s_scr[...]


def _delta(qkv, gb, zsrc, z_col_block, s0, w_onorm, n_seq, n_chunks, c, row_block, aux_block, s0_index, name,
           heads_per_group=N_HEADS):
    kern = functools.partial(_delta_kernel, c=c, heads_per_group=heads_per_group)
    rows_out = n_seq * n_chunks * c
    return pl.pallas_call(
        kern,
        grid=(n_seq, n_chunks),
        in_specs=[
            pl.BlockSpec((c, DN_WIDTH), lambda n, ci: (row_block(n, ci), 0)),
            pl.BlockSpec((c, DN_WIDTH), lambda n, ci: (row_block(n, ci), 1)),
            pl.BlockSpec((c, DN_WIDTH), lambda n, ci: (row_block(n, ci), 2)),
            pl.BlockSpec((c, LANES), lambda n, ci: (aux_block(n, ci), 0)),
            pl.BlockSpec((c, DN_WIDTH), lambda n, ci: (aux_block(n, ci), z_col_block)),
            pl.BlockSpec((1, N_HEADS, HEAD_DIM, HEAD_DIM), lambda n, ci: (s0_index(n), 0, 0, 0)),
            pl.BlockSpec((1, HEAD_DIM), lambda n, ci: (0, 0)),
        ],
        out_specs=[
            pl.BlockSpec((c, DN_WIDTH), lambda n, ci: (n * n_chunks + ci, 0)),
            pl.BlockSpec((1, N_HEADS, HEAD_DIM, HEAD_DIM), lambda n, ci: (n, 0, 0, 0)),
        ],
        out_shape=[
            jax.ShapeDtypeStruct((rows_out, DN_WIDTH), BF16),
            jax.ShapeDtypeStruct((n_seq, N_HEADS, HEAD_DIM, HEAD_DIM), F32),
        ],
        scratch_shapes=[pltpu.VMEM((N_HEADS, HEAD_DIM, HEAD_DIM), F32)],
        compiler_params=_cp("parallel", "arbitrary"),
        name=name,
    )(qkv, qkv, qkv, gb, zsrc, s0, w_onorm.reshape(1, HEAD_DIM))


def _layernorm_silu(c, lnw, lnb):
    mu = jnp.mean(c, axis=-1, keepdims=True)
    xc = c - mu
    y = xc * lax.rsqrt(jnp.mean(xc * xc, axis=-1, keepdims=True) + EPS)
    return _silu(y * lnw + lnb)


def _conf_long_kernel(cur_ref, prev_ref, init_ref, w_ref, b_ref, lnw_ref, lnb_ref, o_ref, ext_ref, c_ref, *, rows, blocks_per_seq):
    hist = CONF_KERNEL - 1
    first = (pl.program_id(0) % blocks_per_seq) == 0
    ext_ref[0:32, :] = jnp.where(first, init_ref[...], prev_ref[...])
    ext_ref[32:, :] = cur_ref[...]
    cw = cur_ref.shape[1]
    lc = min(CONF_LANE_CHUNK, cw)
    rowid = lax.broadcasted_iota(jnp.int32, (8, lc), 0)
    for c0 in range(0, cw, lc):
        cs = slice(c0, c0 + lc)

        def qtile(n, r):
            acc = None
            for a in range(5):
                i = 8 * a + r - 2
                if 0 <= i <= hist:
                    term = w_ref[8 * i:8 * i + 8, cs] * ext_ref[8 * (n + a):8 * (n + a) + 8, cs]
                    acc = term if acc is None else acc + term
            return acc

        held = [None] + [qtile(0, r) for r in range(1, 8)]
        for m in range(rows // 8):
            out = qtile(m, 0)
            for r in range(1, 8):
                nxt = qtile(m + 1, r)
                out = out + pltpu.roll(jnp.where(rowid >= r, held[r], nxt), 8 - r, axis=0)
                held[r] = nxt
            c_ref[8 * m:8 * m + 8, cs] = out
    o_ref[...] = _layernorm_silu(c_ref[...] + b_ref[...], lnw_ref[...], lnb_ref[...]).astype(o_ref.dtype)


def _conf_long(glu, init_hist, w_dw, b_dw, ln_w, ln_b, n_rows, rows, row_block0, blocks_per_seq, prev_index, name):
    cw = glu.shape[1]
    nblk = n_rows // rows
    kern = functools.partial(_conf_long_kernel, rows=rows, blocks_per_seq=blocks_per_seq)
    vec = lambda a: a.reshape(1, cw)
    return pl.pallas_call(
        kern,
        grid=(nblk,),
        in_specs=[
            pl.BlockSpec((rows, cw), lambda i: (i + row_block0, 0)),
            pl.BlockSpec((32, cw), lambda i: (prev_index(i), 0)),
            pl.BlockSpec((32, cw), lambda i: (0, 0)),
            pl.BlockSpec((8 * CONF_KERNEL, cw), lambda i: (0, 0)),
            pl.BlockSpec((1, cw), lambda i: (0, 0)),
            pl.BlockSpec((1, cw), lambda i: (0, 0)),
            pl.BlockSpec((1, cw), lambda i: (0, 0)),
        ],
        out_specs=pl.BlockSpec((rows, cw), lambda i: (i, 0)),
        out_shape=jax.ShapeDtypeStruct((n_rows, cw), BF16),
        scratch_shapes=[pltpu.VMEM((rows + 32, cw), F32), pltpu.VMEM((rows, cw), F32)],
        compiler_params=_cp("parallel"),
        name=name,
    )(glu, glu, init_hist, jnp.repeat(w_dw, 8, axis=0), vec(b_dw), vec(ln_w), vec(ln_b))


def _conf_short_kernel(x_ref, hist_ref, w_ref, b_ref, lnw_ref, lnb_ref, o_ref, *, steps):
    nh = CONF_KERNEL - 1
    for t in range(steps):
        acc = jnp.zeros(x_ref.shape[1:], F32)
        for i in range(CONF_KERNEL):
            j = t + i
            src = hist_ref[j] if j < nh else x_ref[j - nh]
            acc = acc + w_ref[i:i + 1, :] * src
        o_ref[t] = _layernorm_silu(acc + b_ref[...], lnw_ref[...], lnb_ref[...]).astype(o_ref.dtype)


def _conf_short(x_tm, hist_tm, w_dw, b_dw, ln_w, ln_b, nb=32):
    steps, n, cw = x_tm.shape
    kern = functools.partial(_conf_short_kernel, steps=steps)
    vec = lambda a: a.reshape(1, cw)
    return pl.pallas_call(
        kern,
        grid=(n // nb,),
        in_specs=[
            pl.BlockSpec((steps, nb, cw), lambda i: (0, i, 0)),
            pl.BlockSpec((CONF_KERNEL - 1, nb, cw), lambda i: (0, i, 0)),
            pl.BlockSpec((CONF_KERNEL, cw), lambda i: (0, 0)),
            pl.BlockSpec((1, cw), lambda i: (0, 0)),
            pl.BlockSpec((1, cw), lambda i: (0, 0)),
            pl.BlockSpec((1, cw), lambda i: (0, 0)),
        ],
        out_specs=pl.BlockSpec((steps, nb, cw), lambda i: (0, i, 0)),
        out_shape=jax.ShapeDtypeStruct((steps, n, cw), BF16),
        compiler_params=_cp("parallel"),
        name="conf_sample",
    )(x_tm, hist_tm, w_dw, vec(b_dw), vec(ln_w), vec(ln_b))


def _merge_kernel(oz_ref, cz_ref, ga_ref, gb_ref, wa_ref, wb_ref, bias_ref, o_ref):
    ya = jnp.dot(oz_ref[...], wa_ref[...], preferred_element_type=F32)
    yb = jnp.dot(cz_ref[...], wb_ref[...], preferred_element_type=F32) + bias_ref[...]
    o_ref[...] = (ga_ref[...].astype(F32) * ya + gb_ref[...].astype(F32) * yb).astype(o_ref.dtype)


def _merge(oz, cz, gates, wa, wb, bias):
    t, ka = oz.shape
    kb = cz.shape[1]
    d = wa.shape[1]
    nb = d // TN
    return pl.pallas_call(
        _merge_kernel,
        grid=(pl.cdiv(t, TM), nb),
        in_specs=[
            pl.BlockSpec((TM, ka), lambda i, j: (i, 0)),
            pl.BlockSpec((TM, kb), lambda i, j: (i, 0)),
            pl.BlockSpec((TM, TN), lambda i, j: (i, j)),
            pl.BlockSpec((TM, TN), lambda i, j: (i, j + nb)),
            pl.BlockSpec((ka, TN), lambda i, j: (0, j)),
            pl.BlockSpec((kb, TN), lambda i, j: (0, j)),
            pl.BlockSpec((1, TN), lambda i, j: (0, j)),
        ],
        out_specs=pl.BlockSpec((TM, TN), lambda i, j: (i, j)),
        out_shape=jax.ShapeDtypeStruct((t, d), BF16),
        compiler_params=_cp("parallel", "arbitrary"),
        name="merge",
    )(oz, cz, gates, gates, wa, wb, bias.reshape(1, d))


def _out_kernel(m_ref, w_ref, x_ref, o_ref):
    o_ref[...] = x_ref[...] + jnp.dot(m_ref[...], w_ref[...], preferred_element_type=F32)


def _out_proj(merged, w_out, x):
    t, k = merged.shape
    d = w_out.shape[1]
    return pl.pallas_call(
        _out_kernel,
        grid=(pl.cdiv(t, TM), d // TN),
        in_specs=[
            pl.BlockSpec((TM, k), lambda i, j: (i, 0)),
            pl.BlockSpec((k, TN), lambda i, j: (0, j)),
            pl.BlockSpec((TM, TN), lambda i, j: (i, j)),
        ],
        out_specs=pl.BlockSpec((TM, TN), lambda i, j: (i, j)),
        out_shape=jax.ShapeDtypeStruct((t, d), F32),
        compiler_params=_cp("parallel", "arbitrary"),
        name="out_proj",
    )(merged, w_out, x)


def _pack_bf16_pairs(x):
    half = x.shape[1] // 2
    bits = lax.bitcast_convert_type(x.astype(BF16).astype(F32), jnp.uint32)
    return (bits[:, :half] >> 16) | (bits[:, half:] & jnp.uint32(0xFFFF0000))


def _unpack_bf16_pairs(w):
    lo = lax.bitcast_convert_type(w << 16, F32).astype(BF16)
    hi = lax.bitcast_convert_type(w & jnp.uint32(0xFFFF0000), F32).astype(BF16)
    return lo, hi


def _router_kernel(h_ref, nw_ref, wr_ref, br_ref, xn_ref, ids_ref, wts_ref):
    x = h_ref[...]
    xn = x * lax.rsqrt(jnp.mean(x * x, axis=-1, keepdims=True) + EPS) * nw_ref[...]
    xn_ref[...] = _pack_bf16_pairs(xn)
    logits = _hdot(xn, wr_ref[...]) + br_ref[...]
    lane = lax.broadcasted_iota(jnp.int32, logits.shape, 1)
    neg = -jnp.inf
    big = jnp.int32(1 << 20)
    gl = jnp.where(lane < N_GROUPS, logits, neg)
    gmax = jnp.max(gl, axis=-1, keepdims=True)
    gsum = jnp.sum(jnp.exp(gl - gmax), axis=-1, keepdims=True)
    pg_top = 1.0 / gsum
    gidx = jnp.min(jnp.where(gl == gmax, lane, big), axis=-1, keepdims=True)
    lo = N_GROUPS + gidx * EXPERTS_PER_GROUP
    in_grp = (lane >= lo) & (lane < lo + EXPERTS_PER_GROUP)
    el = jnp.where(in_grp, logits, neg)
    emax = jnp.max(el, axis=-1, keepdims=True)
    ex = jnp.exp(el - emax)
    esum = jnp.sum(ex, axis=-1, keepdims=True)
    pe = ex / esum
    pe = jnp.where(in_grp, pe, -1.0)
    p1 = jnp.max(pe, axis=-1, keepdims=True)
    i1 = jnp.min(jnp.where(pe == p1, lane, big), axis=-1, keepdims=True)
    pe2 = jnp.where(lane == i1, -1.0, pe)
    p2 = jnp.max(pe2, axis=-1, keepdims=True)
    i2 = jnp.min(jnp.where(pe2 == p2, lane, big), axis=-1, keepdims=True)
    denom = p1 + p2
    w1 = pg_top * p1 / denom
    w2 = pg_top * p2 / denom
    ids_ref[...] = jnp.where(lane == 0, i1 - N_GROUPS, jnp.where(lane == 1, i2 - N_GROUPS, 0))
    wts_ref[...] = jnp.where(lane == 0, w1, jnp.where(lane == 1, w2, 0.0))


def _router(h1, norm_w, w_r, b_r, rows=256):
    t, d = h1.shape
    return pl.pallas_call(
        _router_kernel,
        grid=(pl.cdiv(t, rows),),
        in_specs=[
            pl.BlockSpec((rows, d), lambda i: (i, 0)),
            pl.BlockSpec((1, d), lambda i: (0, 0)),
            pl.BlockSpec((d, LANES), lambda i: (0, 0)),
            pl.BlockSpec((1, LANES), lambda i: (0, 0)),
        ],
        out_specs=[
            pl.BlockSpec((rows, d // 2), lambda i: (i, 0)),
            pl.BlockSpec((rows, LANES), lambda i: (i, 0)),
            pl.BlockSpec((rows, LANES), lambda i: (i, 0)),
        ],
        out_shape=[
            jax.ShapeDtypeStruct((t, d // 2), jnp.uint32),
            jax.ShapeDtypeStruct((t, LANES), jnp.int32),
            jax.ShapeDtypeStruct((t, LANES), F32),
        ],
        compiler_params=_cp("parallel"),
        name="router",
    )(h1, norm_w.reshape(1, d), w_r, b_r)


def _experts_kernel(ve_ref, nv_ref, vr_ref, slot_ref, xn_hbm, wg_ref, wu_ref, wd_ref, eo_hbm,
                    xbuf, xlo, xhi, acc_ref, gsem, ssem):
    v = pl.program_id(0)
    f = pl.program_id(1)
    nv = nv_ref[0]
    half = xbuf.shape[1]

    def gather_copy(visit, r):
        tok = slot_ref[visit * GROUP_ROWS + r] >> 1
        return pltpu.make_async_copy(xn_hbm.at[pl.ds(tok, 1), :], xbuf.at[pl.ds(r, 1), :], gsem)

    def scatter_copy(visit, r):
        slot = slot_ref[visit * GROUP_ROWS + r]
        return pltpu.make_async_copy(acc_ref.at[pl.ds(r, 1), :], eo_hbm.at[pl.ds(slot, 1), :], ssem)

    def for_rows(visit, fn):
        def body(r, carry):
            fn(visit, r)
            return carry
        lax.fori_loop(0, vr_ref[visit], body, 0)

    start_gather = lambda visit: for_rows(visit, lambda vv, r: gather_copy(vv, r).start())
    wait_gather = lambda visit: for_rows(visit, lambda vv, r: gather_copy(vv, r).wait())
    start_scatter = lambda visit: for_rows(visit, lambda vv, r: scatter_copy(vv, r).start())
    wait_scatter = lambda visit: for_rows(visit, lambda vv, r: scatter_copy(vv, r).wait())

    @pl.when((v == 0) & (f == 0))
    def _():
        xbuf[...] = jnp.zeros_like(xbuf)
        start_gather(0)

    @pl.when((v < nv) & (f == 0))
    def _():
        wait_gather(v)
        lo, hi = _unpack_bf16_pairs(xbuf[...])
        xlo[...] = lo
        xhi[...] = hi

        @pl.when(v + 1 < nv)
        def _():
            start_gather(v + 1)

    @pl.when(v < nv)
    def _():
        wg = wg_ref[0].astype(BF16)
        wu = wu_ref[0].astype(BF16)
        xa = xlo[...]
        xb = xhi[...]
        g = jnp.dot(xa, wg[:half], preferred_element_type=F32) + jnp.dot(xb, wg[half:], preferred_element_type=F32)
        u = jnp.dot(xa, wu[:half], preferred_element_type=F32) + jnp.dot(xb, wu[half:], preferred_element_type=F32)
        hmid = (_silu(g) * u).astype(BF16)
        part = jnp.dot(hmid, wd_ref[0].astype(BF16), preferred_element_type=F32)

        @pl.when(f == 0)
        def _():
            @pl.when(v > 0)
            def _():
                wait_scatter(v - 1)
            acc_ref[...] = part

        @pl.when(f > 0)
        def _():
            acc_ref[...] += part

        @pl.when(f == pl.num_programs(1) - 1)
        def _():
            start_scatter(v)

            @pl.when(v == nv - 1)
            def _():
                wait_scatter(v)


def _experts(visit_expert, n_visits, visit_rows, row_slot, xn_packed, w_gate, w_up, w_down, max_visits):
    t, half = xn_packed.shape
    d = 2 * half
    fdim = w_gate.shape[2]
    nf = fdim // EXPERT_F_TILE
    ftile = lambda v, f, nv: jnp.where(v < nv[0], f, nf - 1)
    grid_spec = pltpu.PrefetchScalarGridSpec(
        num_scalar_prefetch=4,
        grid=(max_visits, nf),
        in_specs=[
            pl.BlockSpec(memory_space=pl.ANY),
            pl.BlockSpec((1, d, EXPERT_F_TILE), lambda v, f, ve, nv, vr, sl: (ve[v], 0, ftile(v, f, nv))),
            pl.BlockSpec((1, d, EXPERT_F_TILE), lambda v, f, ve, nv, vr, sl: (ve[v], 0, ftile(v, f, nv))),
            pl.BlockSpec((1, EXPERT_F_TILE, d), lambda v, f, ve, nv, vr, sl: (ve[v], ftile(v, f, nv), 0)),
        ],
        out_specs=pl.BlockSpec(memory_space=pl.ANY),
        scratch_shapes=[
            pltpu.VMEM((GROUP_ROWS, half), jnp.uint32),
            pltpu.VMEM((GROUP_ROWS, half), BF16),
            pltpu.VMEM((GROUP_ROWS, half), BF16),
            pltpu.VMEM((GROUP_ROWS, d), F32),
            pltpu.SemaphoreType.DMA(()),
            pltpu.SemaphoreType.DMA(()),
        ],
    )
    return pl.pallas_call(
        _experts_kernel,
        grid_spec=grid_spec,
        out_shape=jax.ShapeDtypeStruct((2 * t, d), F32),
        compiler_params=_cp("arbitrary", "arbitrary"),
        name="experts",
    )(visit_expert, n_visits, visit_rows, row_slot, xn_packed, w_gate, w_up, w_down)


def _final_kernel(h_ref, o1_ref, o2_ref, wts_ref, nw_ref, y_ref):
    wts = wts_ref[...]
    h = h_ref[...] + wts[:, 0:1] * o1_ref[...] + wts[:, 1:2] * o2_ref[...]
    y_ref[...] = h * lax.rsqrt(jnp.mean(h * h, axis=-1, keepdims=True) + EPS) * nw_ref[...]


def _final(h1, eo2, wts, norm_w, n_rows, row_block0, rows=FINAL_ROWS):
    d = h1.shape[1]
    rb = lambda i: (i + row_block0, 0)
    return pl.pallas_call(
        _final_kernel,
        grid=(n_rows // rows,),
        in_specs=[
            pl.BlockSpec((rows, d), rb),
            pl.BlockSpec((rows, d), rb),
            pl.BlockSpec((rows, d), lambda i: (i + row_block0, 1)),
            pl.BlockSpec((rows, LANES), rb),
            pl.BlockSpec((1, d), lambda i: (0, 0)),
        ],
        out_specs=pl.BlockSpec((rows, d), lambda i: (i, 0)),
        out_shape=jax.ShapeDtypeStruct((n_rows, d), F32),
        compiler_params=_cp("parallel"),
        name="final",
    )(h1, eo2, eo2, wts, norm_w.reshape(1, d))


def _dispatch_plan(ids, max_visits):
    flat_e = ids[:, :2].reshape(-1)
    n_slots = flat_e.shape[0]
    order = jnp.argsort(flat_e, stable=True).astype(jnp.int32)
    sorted_e = flat_e[order]
    counts = jnp.zeros((N_EXPERTS,), jnp.int32).at[flat_e].add(1)
    tiles = (counts + GROUP_ROWS - 1) // GROUP_ROWS
    tile_end = jnp.cumsum(tiles)
    tile_start = tile_end - tiles
    group_start = jnp.cumsum(counts) - counts
    rank = jnp.arange(n_slots, dtype=jnp.int32) - group_start[sorted_e]
    padded_pos = tile_start[sorted_e] * GROUP_ROWS + rank
    n_visits = tile_end[-1]
    row_slot = jnp.zeros((max_visits * GROUP_ROWS,), jnp.int32).at[padded_pos].set(order)
    visit = jnp.arange(max_visits, dtype=jnp.int32)
    visit_expert = jnp.searchsorted(tile_end, jnp.minimum(visit, n_visits - 1), side="right").astype(jnp.int32)
    visit_expert = jnp.minimum(visit_expert, N_EXPERTS - 1)
    rows_left = counts[visit_expert] - GROUP_ROWS * (visit - tile_start[visit_expert])
    visit_rows = jnp.where(visit < n_visits, jnp.clip(rows_left, 0, GROUP_ROWS), 0).astype(jnp.int32)
    return row_slot, visit_expert, n_visits.reshape(1).astype(jnp.int32), visit_rows


def kernel(x_prompt, x_sample, state_delta, state_conv_qkv, state_conv_b, meta_tokens,
           norm1, w_in, w_conv_qkv, a_log, dt_bias, w_onorm, w_proj_a, w_dw, b_dw, ln_w, ln_b,
           w_proj_b, b_proj_b, w_out, norm2, w_rg, b_rg, w_re, b_re, w_e_gate, w_e_up, w_e_down,
           final_norm):
    bsz, seq, d = x_prompt.shape
    nsmp, steps, _ = x_sample.shape
    n_meta = meta_tokens.shape[0]
    depth = norm1.shape[0]
    assert depth == 1
    tp = bsz * seq
    ts = nsmp * steps
    t_all = tp + ts + n_meta
    qkv_w = 3 * DN_WIDTH
    conf_w = w_dw.shape[-1]
    hist = CONF_KERNEL - 1

    x_all = jnp.concatenate([x_prompt.reshape(tp, d), x_sample.transpose(1, 0, 2).reshape(ts, d), meta_tokens], axis=0)

    wi = w_in[0]
    o_z = qkv_w + DN_WIDTH
    o_glu = o_z + 2 * N_HEADS
    o_gate = o_glu + 2 * conf_w
    w_qkvz = wi[:, :o_z].astype(BF16)
    w_ab = jnp.pad(wi[:, o_z:o_glu], ((0, 0), (0, LANES - 2 * N_HEADS))).astype(BF16)
    w_glu = wi[:, o_glu:o_gate].astype(BF16)
    w_gate = wi[:, o_gate:].astype(BF16)
    alog_pad = jnp.pad(a_log[0], (0, LANES - N_HEADS)).reshape(1, LANES)
    dtb_pad = jnp.pad(dt_bias[0], (0, LANES - N_HEADS)).reshape(1, LANES)

    hn = _rmsnorm_bf16(x_all, norm1[0])
    qkvz = _proj(_mm_plain_kernel, hn, w_qkvz, o_z, F32, "proj_qkvz")
    gbeta = _proj_decay(hn, w_ab, alog_pad, dtb_pad)
    glu = _proj_glu(hn, w_glu)
    gates = _proj(_mm_sigmoid_kernel, hn, w_gate, 2 * d, BF16, "proj_gates")

    wc = w_conv_qkv[0]
    meta_blk = (tp + ts) // n_meta
    qkv_meta = _prep_long(qkvz, wc, n_meta, n_meta, meta_blk, lambda i: 0, True, "prep_meta")
    rows_p = 256
    bps = seq // rows_p
    meta_halo = (tp + ts + n_meta) // 8 - 1
    qkv_p = _prep_long(qkvz, wc, tp, rows_p, 0,
                       lambda i: jnp.where(i % bps == 0, meta_halo, i * (rows_p // 8) - 1), False, "prep_prompt")
    raw_s = qkvz[tp:tp + ts, :qkv_w].reshape(steps, nsmp, qkv_w)
    st_qkv_tm = state_conv_qkv[0].transpose(1, 0, 2)
    qkv_s_tm = _prep_short(raw_s, st_qkv_tm, wc)

    zero_state = jnp.zeros((1, N_HEADS, HEAD_DIM, HEAD_DIM), F32)
    oz_meta, s_meta = _delta(qkv_meta, gbeta, qkvz, 3, zero_state, w_onorm[0], 1, 1, n_meta,
                             lambda n, ci: 0, lambda n, ci: meta_blk, lambda n: 0, "delta_meta")
    n_chunks = seq // CHUNK
    prompt_blk = lambda n, ci: n * n_chunks + ci
    oz_p, s_p = _delta(qkv_p, gbeta, qkvz, 3, s_meta, w_onorm[0], bsz, n_chunks, CHUNK,
                       prompt_blk, prompt_blk, lambda n: 0, "delta_prompt")

    cpad = 8
    to_bm = lambda a: jnp.pad(a.transpose(1, 0, 2), ((0, 0), (0, cpad - steps), (0, 0))).reshape(nsmp * cpad, a.shape[-1])
    qkv_s = to_bm(qkv_s_tm)
    gb_s = to_bm(gbeta[tp:tp + ts].reshape(steps, nsmp, LANES))
    z_s = to_bm(qkvz[tp:tp + ts, qkv_w:].reshape(steps, nsmp, DN_WIDTH))
    oz_s_bm, s_s = _delta(qkv_s, gb_s, z_s, 0, state_delta[0], w_onorm[0], nsmp, 1, cpad,
                          lambda n, ci: n, lambda n, ci: n, lambda n: n, "delta_sample")
    oz_s = oz_s_bm.reshape(nsmp, cpad, DN_WIDTH)[:, :steps].transpose(1, 0, 2).reshape(ts, DN_WIDTH)
    oz = jnp.concatenate([oz_p, oz_s, oz_meta], axis=0)

    zeros32 = jnp.zeros((32, conf_w), F32)
    cz_meta = _conf_long(glu, zeros32, w_dw[0], b_dw[0], ln_w[0], ln_b[0], n_meta, n_meta, meta_blk, 1,
                         lambda i: 0, "conf_meta")
    init_p = jnp.concatenate([jnp.zeros((32 - n_meta, conf_w), F32), glu[tp + ts:]], axis=0)
    rows_c = 128
    cz_p = _conf_long(glu, init_p, w_dw[0], b_dw[0], ln_w[0], ln_b[0], tp, rows_c, 0, seq // rows_c,
                      lambda i: jnp.maximum(i * (rows_c // 32) - 1, 0), "conf_prompt")
    glu_s_tm = glu[tp:tp + ts].reshape(steps, nsmp, conf_w)
    hist_s_tm = state_conv_b[0].transpose(1, 0, 2)
    cz_s = _conf_short(glu_s_tm, hist_s_tm, w_dw[0], b_dw[0], ln_w[0], ln_b[0]).reshape(ts, conf_w)
    cz = jnp.concatenate([cz_p, cz_s, cz_meta], axis=0)

    merged = _merge(oz, cz, gates, w_proj_a[0].astype(BF16), w_proj_b[0].astype(BF16), b_proj_b[0])
    h1 = _out_proj(merged, w_out[0].astype(BF16), x_all)

    w_r = jnp.pad(jnp.concatenate([w_rg[0], w_re[0]], axis=1), ((0, 0), (0, LANES - N_GROUPS - N_EXPERTS)))
    b_r = jnp.pad(jnp.concatenate([b_rg[0], b_re[0]]), (0, LANES - N_GROUPS - N_EXPERTS)).reshape(1, LANES)
    xn2, ids, wts = _router(h1, norm2[0], w_r, b_r)
    max_visits = N_EXPERTS + (2 * t_all) // GROUP_ROWS
    row_slot, visit_expert, n_visits, visit_rows = _dispatch_plan(ids, max_visits)
    eo = _experts(visit_expert, n_visits, visit_rows, row_slot, xn2, w_e_gate[0], w_e_up[0], w_e_down[0], max_visits)
    eo2 = eo.reshape(t_all, 2 * d)
    y_p = _final(h1, eo2, wts, final_norm, tp, 0)
    y_s = _final(h1, eo2, wts, final_norm, ts, tp // FINAL_ROWS)

    y_prompt = y_p.reshape(bsz, seq, d)
    y_sample = y_s.reshape(steps, nsmp, d).transpose(1, 0, 2)
    new_cq_p = qkvz[:tp, :qkv_w].reshape(bsz, seq, qkv_w)[:, seq - (SHORT_CONV - 1):]
    new_cb_p = glu[:tp].reshape(bsz, seq, conf_w)[:, seq - hist:]
    cq_s = jnp.concatenate([st_qkv_tm, raw_s], axis=0)[-(SHORT_CONV - 1):].transpose(1, 0, 2)
    cb_s = jnp.concatenate([hist_s_tm, glu_s_tm], axis=0)[-hist:].transpose(1, 0, 2)
    return (y_prompt, y_sample, s_p[None], new_cq_p[None], new_cb_p[None], s_s[None], cq_s[None], cb_s[None])
```

```python
import functools

import jax
import jax.numpy as jnp
from jax import lax
from jax.experimental import pallas as pl
from jax.experimental.pallas import tpu as pltpu

F32 = jnp.float32
BF16 = jnp.bfloat16
EPS = 1e-6

N_HEADS = 16
HEAD_DIM = 128
DN_WIDTH = N_HEADS * HEAD_DIM
SHORT_CONV = 4
CONF_KERNEL = 31
N_GROUPS = 8
EXPERTS_PER_GROUP = 8
N_EXPERTS = N_GROUPS * EXPERTS_PER_GROUP
CHUNK = 64
LANES = 128
HIGHEST = lax.Precision.HIGHEST

TM = 1024
TN = 512
GROUP_ROWS = 320
EXPERT_F_TILE = 256
DMA_UNROLL = 8
CONF_LANE_CHUNK = 512
FINAL_ROWS = 256
VMEM_LIMIT = 56 * 1024 * 1024


def _cp(*sem):
    return pltpu.CompilerParams(dimension_semantics=sem, vmem_limit_bytes=VMEM_LIMIT)


def _sigmoid(x):
    return 1.0 / (1.0 + jnp.exp(-x))


def _silu(x):
    return x * _sigmoid(x)


def _softplus(x):
    return jnp.maximum(x, 0.0) + jnp.log1p(jnp.exp(-jnp.abs(x)))


def _rmsnorm_kernel(xp_ref, xt_ref, w_ref, o_ref, *, prompt_blocks):
    def norm(x):
        ms = jnp.mean(x * x, axis=-1, keepdims=True)
        return (x * lax.rsqrt(ms + EPS) * w_ref[...]).astype(o_ref.dtype)

    @pl.when(pl.program_id(0) < prompt_blocks)
    def _():
        o_ref[...] = norm(xp_ref[...])

    @pl.when(pl.program_id(0) >= prompt_blocks)
    def _():
        o_ref[...] = norm(xt_ref[...])


def _rmsnorm_bf16(x_prompt, x_tail, w, t_all, rows=256):
    tp, d = x_prompt.shape
    nbp = tp // rows
    return pl.pallas_call(
        functools.partial(_rmsnorm_kernel, prompt_blocks=nbp),
        grid=(pl.cdiv(t_all, rows),),
        in_specs=[
            pl.BlockSpec((rows, d), lambda i: (jnp.minimum(i, nbp - 1), 0)),
            pl.BlockSpec((rows, d), lambda i: (jnp.maximum(i - nbp, 0), 0)),
            pl.BlockSpec((1, d), lambda i: (0, 0)),
        ],
        out_specs=pl.BlockSpec((rows, d), lambda i: (i, 0)),
        out_shape=jax.ShapeDtypeStruct((t_all, d), BF16),
        compiler_params=_cp("parallel"),
        name="rmsnorm1",
    )(x_prompt, x_tail, w.reshape(1, d))


def _mm_plain_kernel(x_ref, w_ref, o_ref):
    o_ref[...] = jnp.dot(x_ref[...], w_ref[...], preferred_element_type=F32).astype(o_ref.dtype)


def _mm_sigmoid_kernel(x_ref, w_ref, o_ref):
    o_ref[...] = _sigmoid(jnp.dot(x_ref[...], w_ref[...], preferred_element_type=F32)).astype(o_ref.dtype)


def _mm_glu_kernel(x_ref, wu_ref, wg_ref, o_ref):
    x = x_ref[...]
    u = jnp.dot(x, wu_ref[...], preferred_element_type=F32)
    g = jnp.dot(x, wg_ref[...], preferred_element_type=F32)
    o_ref[...] = u * _sigmoid(g)


def _mm_decay_kernel(x_ref, w_ref, alog_ref, dtb_ref, o_ref):
    acc = jnp.dot(x_ref[...], w_ref[...], preferred_element_type=F32)
    lane = lax.broadcasted_iota(jnp.int32, acc.shape, 1)
    g = -jnp.exp(alog_ref[...]) * _softplus(acc + dtb_ref[...])
    beta = _sigmoid(acc)
    o_ref[...] = jnp.where(lane < N_HEADS, g, jnp.where(lane < 2 * N_HEADS, beta, 0.0))


def _proj(kernel_fn, hn, w, n_out, out_dtype, name, tn=TN):
    t, k = hn.shape
    return pl.pallas_call(
        kernel_fn,
        grid=(pl.cdiv(t, TM), n_out // tn),
        in_specs=[pl.BlockSpec((TM, k), lambda i, j: (i, 0)), pl.BlockSpec((k, tn), lambda i, j: (0, j))],
        out_specs=pl.BlockSpec((TM, tn), lambda i, j: (i, j)),
        out_shape=jax.ShapeDtypeStruct((t, n_out), out_dtype),
        compiler_params=_cp("parallel", "arbitrary"),
        name=name,
    )(hn, w)


def _proj_glu(hn, w_glu):
    t, k = hn.shape
    half = w_glu.shape[1] // 2
    nb = half // TN
    return pl.pallas_call(
        _mm_glu_kernel,
        grid=(pl.cdiv(t, TM), nb),
        in_specs=[
            pl.BlockSpec((TM, k), lambda i, j: (i, 0)),
            pl.BlockSpec((k, TN), lambda i, j: (0, j)),
            pl.BlockSpec((k, TN), lambda i, j: (0, j + nb)),
        ],
        out_specs=pl.BlockSpec((TM, TN), lambda i, j: (i, j)),
        out_shape=jax.ShapeDtypeStruct((t, half), F32),
        compiler_params=_cp("parallel", "arbitrary"),
        name="proj_glu",
    )(hn, w_glu, w_glu)


def _proj_decay(hn, w_ab, alog_pad, dtb_pad):
    t, k = hn.shape
    return pl.pallas_call(
        _mm_decay_kernel,
        grid=(pl.cdiv(t, TM),),
        in_specs=[
            pl.BlockSpec((TM, k), lambda i: (i, 0)),
            pl.BlockSpec((k, LANES), lambda i: (0, 0)),
            pl.BlockSpec((1, LANES), lambda i: (0, 0)),
            pl.BlockSpec((1, LANES), lambda i: (0, 0)),
        ],
        out_specs=pl.BlockSpec((TM, LANES), lambda i: (i, 0)),
        out_shape=jax.ShapeDtypeStruct((t, LANES), F32),
        compiler_params=_cp("parallel"),
        name="proj_decay",
    )(hn, w_ab, alog_pad, dtb_pad)


def _head_normalize(y, sec):
    outs = []
    for h in range(N_HEADS):
        yh = y[:, h * HEAD_DIM:(h + 1) * HEAD_DIM]
        inv = lax.rsqrt(jnp.sum(yh * yh, axis=-1, keepdims=True) + EPS)
        scale = jnp.where(sec == 0, inv * (HEAD_DIM ** -0.5), jnp.where(sec == 1, inv, 1.0))
        outs.append(yh * scale)
    return jnp.concatenate(outs, axis=-1)


def _prep_long_kernel(cur_ref, halo_ref, w_ref, o_ref, ext_ref, *, rows, zero_halo):
    sec = pl.program_id(1)
    halo = halo_ref[...]
    ext_ref[0:8, :] = jnp.zeros_like(halo) if zero_halo else halo
    ext_ref[8:, :] = cur_ref[...]
    acc = w_ref[SHORT_CONV - 1:SHORT_CONV, :] * cur_ref[...]
    for s in range(1, SHORT_CONV):
        acc = acc + w_ref[SHORT_CONV - 1 - s:SHORT_CONV - s, :] * ext_ref[pl.ds(8 - s, rows), :]
    o_ref[...] = _head_normalize(_silu(acc), sec)


def _prep_long(qkvz, w_conv, n_rows, rows, row_block0, halo_index, zero_halo, name):
    nblk = n_rows // rows
    kern = functools.partial(_prep_long_kernel, rows=rows, zero_halo=zero_halo)
    return pl.pallas_call(
        kern,
        grid=(nblk, 3),
        in_specs=[
            pl.BlockSpec((rows, DN_WIDTH), lambda i, s: (i + row_block0, s)),
            pl.BlockSpec((8, DN_WIDTH), lambda i, s: (halo_index(i), s)),
            pl.BlockSpec((SHORT_CONV, DN_WIDTH), lambda i, s: (0, s)),
        ],
        out_specs=pl.BlockSpec((rows, DN_WIDTH), lambda i, s: (i, s)),
        out_shape=jax.ShapeDtypeStruct((n_rows, 3 * DN_WIDTH), F32),
        scratch_shapes=[pltpu.VMEM((rows + 8, DN_WIDTH), F32)],
        compiler_params=_cp("parallel", "arbitrary"),
        name=name,
    )(qkvz, qkvz, w_conv)


def _prep_short_kernel(x_ref, st_ref, w_ref, o_ref, *, steps):
    sec = pl.program_id(0)
    buf = [st_ref[i] for i in range(SHORT_CONV - 1)] + [x_ref[t] for t in range(steps)]
    for t in range(steps):
        acc = w_ref[0:1, :] * buf[t]
        for i in range(1, SHORT_CONV):
            acc = acc + w_ref[i:i + 1, :] * buf[t + i]
        o_ref[t] = _head_normalize(_silu(acc), sec)


def _prep_short(x_tm, state_tm, w_conv):
    steps, n, _ = x_tm.shape
    kern = functools.partial(_prep_short_kernel, steps=steps)
    return pl.pallas_call(
        kern,
        grid=(3,),
        in_specs=[
            pl.BlockSpec((steps, n, DN_WIDTH), lambda s: (0, 0, s)),
            pl.BlockSpec((SHORT_CONV - 1, n, DN_WIDTH), lambda s: (0, 0, s)),
            pl.BlockSpec((SHORT_CONV, DN_WIDTH), lambda s: (0, s)),
        ],
        out_specs=pl.BlockSpec((steps, n, DN_WIDTH), lambda s: (0, 0, s)),
        out_shape=jax.ShapeDtypeStruct((steps, n, 3 * DN_WIDTH), F32),
        compiler_params=_cp("arbitrary"),
        name="prep_sample",
    )(x_tm, state_tm, w_conv)


def _hdot(a, b):
    return jnp.dot(a, b, preferred_element_type=F32, precision=HIGHEST)


def _split_bf16(a):
    hi = a.astype(BF16)
    lo = (a - hi.astype(F32)).astype(BF16)
    return hi, lo


def _bmm(a, b):
    return jnp.einsum("hik,hkj->hij", a.astype(BF16), b.astype(BF16), preferred_element_type=F32)


def _bmm3(a, b):
    ah, al = _split_bf16(a)
    bh, bl = _split_bf16(b)
    f = lambda x, y: jnp.einsum("hik,hkj->hij", x, y, preferred_element_type=F32)
    return f(ah, bh) + (f(ah, bl) + f(al, bh))


def _unit_lower_inverse(low, c):
    base = min(c, 16)
    row = lax.broadcasted_iota(jnp.int32, (c, c), 0)
    col = lax.broadcasted_iota(jnp.int32, (c, c), 1)
    eye = (row == col).astype(F32)[None]
    diag = jnp.where(((row // base) == (col // base))[None], low, 0.0) if c > base else low
    inv = eye - diag
    power = diag
    k = 2
    while k < base:
        power = _bmm3(power, power)
        inv = inv + _bmm3(inv, power)
        k *= 2
    blk = base
    while blk < c:
        sel = ((row // (2 * blk)) == (col // (2 * blk))) & (((row // blk) % 2) == 1) & (((col // blk) % 2) == 0)
        off = jnp.where(sel[None], low, 0.0)
        inv = inv - _bmm3(inv, _bmm3(off, inv))
        blk *= 2
    return inv


def _delta_kernel(q_ref, k_ref, v_ref, gb_ref, z_ref, s0_ref, wn_ref, o_ref, sout_ref, s_scr, *, c, heads_per_group):
    ci = pl.program_id(1)

    @pl.when(ci == 0)
    def _():
        s_scr[...] = s0_ref[0]

    row = lax.broadcasted_iota(jnp.int32, (c, c), 0)
    col = lax.broadcasted_iota(jnp.int32, (c, c), 1)
    causal = (row >= col)[None]
    strict = (row > col)[None]
    gb = gb_ref[...]
    gcum = _hdot((row >= col).astype(F32), gb)
    gcum_t = gcum.T
    wn = wn_ref[...]
    d = HEAD_DIM
    for h0 in range(0, N_HEADS, heads_per_group):
        heads = range(h0, h0 + heads_per_group)
        hs = slice(h0, h0 + heads_per_group)
        heads_of = lambda ref: jnp.stack([ref[:, h * d:(h + 1) * d] for h in heads])
        q = heads_of(q_ref)
        k = heads_of(k_ref)
        v = heads_of(v_ref)
        g_col = jnp.stack([gcum[:, h:h + 1] for h in heads])
        g_row = jnp.stack([gcum_t[h:h + 1, :] for h in heads])
        beta = jnp.stack([gb[:, N_HEADS + h:N_HEADS + h + 1] for h in heads])
        g_last = g_col[:, c - 1:c, :]
        decay = jnp.where(causal, jnp.exp(g_col - g_row), 0.0)
        exp_g = jnp.exp(g_col)
        kb = k * beta
        kk = jnp.einsum("hid,hjd->hij", jnp.concatenate([kb, q], axis=1).astype(BF16), k.astype(BF16),
                        preferred_element_type=F32)
        lower = jnp.where(strict, kk[:, :c] * decay, 0.0)
        attn = kk[:, c:] * decay
        tinv = _unit_lower_inverse(lower, c)
        rhs = jnp.concatenate([v * beta, kb * exp_g], axis=-1)
        sol = _bmm3(tinv, rhs)
        s = s_scr[hs]
        both = _bmm(jnp.concatenate([sol[:, :, d:], q * exp_g], axis=1), s)
        u = sol[:, :, :d] - both[:, :c]
        o = both[:, c:] + _bmm(attn, u)
        kd = k * jnp.exp(g_last - g_col)
        upd = lax.dot_general(kd.astype(BF16), u.astype(BF16), (((1,), (1,)), ((0,), (0,))),
                              preferred_element_type=F32)
        s_scr[hs] = s * jnp.exp(g_last) + upd
        on = o * lax.rsqrt(jnp.mean(o * o, axis=-1, keepdims=True) + EPS) * wn
        out = (on * _silu(heads_of(z_ref))).astype(o_ref.dtype)
        for i, h in enumerate(heads):
            o_ref[:, h * d:(h + 1) * d] = out[i]

    @pl.when(ci == pl.num_programs(1) - 1)
    def _():
        sout_ref[0] = s_scr[...]


def _delta(qkv, gb, zsrc, z_col_block, s0, w_onorm, n_seq, n_chunks, c, row_block, aux_block, s0_index, name,
           heads_per_group=N_HEADS):
    kern = functools.partial(_delta_kernel, c=c, heads_per_group=heads_per_group)
    rows_out = n_seq * n_chunks * c
    return pl.pallas_call(
        kern,
        grid=(n_seq, n_chunks),
        in_specs=[
            pl.BlockSpec((c, DN_WIDTH), lambda n, ci: (row_block(n, ci), 0)),
            pl.BlockSpec((c, DN_WIDTH), lambda n, ci: (row_block(n, ci), 1)),
            pl.BlockSpec((c, DN_WIDTH), lambda n, ci: (row_block(n, ci), 2)),
            pl.BlockSpec((c, LANES), lambda n, ci: (aux_block(n, ci), 0)),
            pl.BlockSpec((c, DN_WIDTH), lambda n, ci: (aux_block(n, ci), z_col_block)),
            pl.BlockSpec((1, N_HEADS, HEAD_DIM, HEAD_DIM), lambda n, ci: (s0_index(n), 0, 0, 0)),
            pl.BlockSpec((1, HEAD_DIM), lambda n, ci: (0, 0)),
        ],
        out_specs=[
            pl.BlockSpec((c, DN_WIDTH), lambda n, ci: (n * n_chunks + ci, 0)),
            pl.BlockSpec((1, N_HEADS, HEAD_DIM, HEAD_DIM), lambda n, ci: (n, 0, 0, 0)),
        ],
        out_shape=[
            jax.ShapeDtypeStruct((rows_out, DN_WIDTH), BF16),
            jax.ShapeDtypeStruct((n_seq, N_HEADS, HEAD_DIM, HEAD_DIM), F32),
        ],
        scratch_shapes=[pltpu.VMEM((N_HEADS, HEAD_DIM, HEAD_DIM), F32)],
        compiler_params=_cp("parallel", "arbitrary"),
        name=name,
    )(qkv, qkv, qkv, gb, zsrc, s0, w_onorm.reshape(1, HEAD_DIM))


def _layernorm_silu(c, lnw, lnb):
    mu = jnp.mean(c, axis=-1, keepdims=True)
    xc = c - mu
    y = xc * lax.rsqrt(jnp.mean(xc * xc, axis=-1, keepdims=True) + EPS)
    return _silu(y * lnw + lnb)


def _conf_long_kernel(cur_ref, prev_ref, init_ref, w_ref, b_ref, lnw_ref, lnb_ref, o_ref, ext_ref, c_ref, *, rows, blocks_per_seq):
    hist = CONF_KERNEL - 1
    first = (pl.program_id(0) % blocks_per_seq) == 0
    ext_ref[0:32, :] = jnp.where(first, init_ref[...], prev_ref[...])
    ext_ref[32:, :] = cur_ref[...]
    cw = cur_ref.shape[1]
    lc = min(CONF_LANE_CHUNK, cw)
    rowid = lax.broadcasted_iota(jnp.int32, (8, lc), 0)
    for c0 in range(0, cw, lc):
        cs = slice(c0, c0 + lc)

        def qtile(n, r):
            acc = None
            for a in range(5):
                i = 8 * a + r - 2
                if 0 <= i <= hist:
                    term = w_ref[8 * i:8 * i + 8, cs] * ext_ref[8 * (n + a):8 * (n + a) + 8, cs]
                    acc = term if acc is None else acc + term
            return acc

        held = [None] + [qtile(0, r) for r in range(1, 8)]
        for m in range(rows // 8):
            out = qtile(m, 0)
            for r in range(1, 8):
                nxt = qtile(m + 1, r)
                out = out + pltpu.roll(jnp.where(rowid >= r, held[r], nxt), 8 - r, axis=0)
                held[r] = nxt
            c_ref[8 * m:8 * m + 8, cs] = out
    o_ref[...] = _layernorm_silu(c_ref[...] + b_ref[...], lnw_ref[...], lnb_ref[...]).astype(o_ref.dtype)


def _conf_long(glu, init_hist, w_dw, b_dw, ln_w, ln_b, n_rows, rows, row_block0, blocks_per_seq, prev_index, name):
    cw = glu.shape[1]
    nblk = n_rows // rows
    kern = functools.partial(_conf_long_kernel, rows=rows, blocks_per_seq=blocks_per_seq)
    vec = lambda a: a.reshape(1, cw)
    return pl.pallas_call(
        kern,
        grid=(nblk,),
        in_specs=[
            pl.BlockSpec((rows, cw), lambda i: (i + row_block0, 0)),
            pl.BlockSpec((32, cw), lambda i: (prev_index(i), 0)),
            pl.BlockSpec((32, cw), lambda i: (0, 0)),
            pl.BlockSpec((8 * CONF_KERNEL, cw), lambda i: (0, 0)),
            pl.BlockSpec((1, cw), lambda i: (0, 0)),
            pl.BlockSpec((1, cw), lambda i: (0, 0)),
            pl.BlockSpec((1, cw), lambda i: (0, 0)),
        ],
        out_specs=pl.BlockSpec((rows, cw), lambda i: (i, 0)),
        out_shape=jax.ShapeDtypeStruct((n_rows, cw), BF16),
        scratch_shapes=[pltpu.VMEM((rows + 32, cw), F32), pltpu.VMEM((rows, cw), F32)],
        compiler_params=_cp("parallel"),
        name=name,
    )(glu, glu, init_hist, jnp.repeat(w_dw, 8, axis=0), vec(b_dw), vec(ln_w), vec(ln_b))


def _conf_short_kernel(x_ref, hist_ref, w_ref, b_ref, lnw_ref, lnb_ref, o_ref, *, steps):
    nh = CONF_KERNEL - 1
    for t in range(steps):
        acc = jnp.zeros(x_ref.shape[1:], F32)
        for i in range(CONF_KERNEL):
            j = t + i
            src = hist_ref[j] if j < nh else x_ref[j - nh]
            acc = acc + w_ref[i:i + 1, :] * src
        o_ref[t] = _layernorm_silu(acc + b_ref[...], lnw_ref[...], lnb_ref[...]).astype(o_ref.dtype)


def _conf_short(x_tm, hist_tm, w_dw, b_dw, ln_w, ln_b, nb=32):
    steps, n, cw = x_tm.shape
    kern = functools.partial(_conf_short_kernel, steps=steps)
    vec = lambda a: a.reshape(1, cw)
    return pl.pallas_call(
        kern,
        grid=(n // nb,),
        in_specs=[
            pl.BlockSpec((steps, nb, cw), lambda i: (0, i, 0)),
            pl.BlockSpec((CONF_KERNEL - 1, nb, cw), lambda i: (0, i, 0)),
            pl.BlockSpec((CONF_KERNEL, cw), lambda i: (0, 0)),
            pl.BlockSpec((1, cw), lambda i: (0, 0)),
            pl.BlockSpec((1, cw), lambda i: (0, 0)),
            pl.BlockSpec((1, cw), lambda i: (0, 0)),
        ],
        out_specs=pl.BlockSpec((steps, nb, cw), lambda i: (0, i, 0)),
        out_shape=jax.ShapeDtypeStruct((steps, n, cw), BF16),
        compiler_params=_cp("parallel"),
        name="conf_sample",
    )(x_tm, hist_tm, w_dw, vec(b_dw), vec(ln_w), vec(ln_b))


def _merge_kernel(oz_ref, cz_ref, ga_ref, gb_ref, wa_ref, wb_ref, bias_ref, o_ref):
    ya = jnp.dot(oz_ref[...], wa_ref[...], preferred_element_type=F32)
    yb = jnp.dot(cz_ref[...], wb_ref[...], preferred_element_type=F32) + bias_ref[...]
    o_ref[...] = (ga_ref[...].astype(F32) * ya + gb_ref[...].astype(F32) * yb).astype(o_ref.dtype)


def _merge(oz, cz, gates, wa, wb, bias):
    t, ka = oz.shape
    kb = cz.shape[1]
    d = wa.shape[1]
    nb = d // TN
    return pl.pallas_call(
        _merge_kernel,
        grid=(pl.cdiv(t, TM), nb),
        in_specs=[
            pl.BlockSpec((TM, ka), lambda i, j: (i, 0)),
            pl.BlockSpec((TM, kb), lambda i, j: (i, 0)),
            pl.BlockSpec((TM, TN), lambda i, j: (i, j)),
            pl.BlockSpec((TM, TN), lambda i, j: (i, j + nb)),
            pl.BlockSpec((ka, TN), lambda i, j: (0, j)),
            pl.BlockSpec((kb, TN), lambda i, j: (0, j)),
            pl.BlockSpec((1, TN), lambda i, j: (0, j)),
        ],
        out_specs=pl.BlockSpec((TM, TN), lambda i, j: (i, j)),
        out_shape=jax.ShapeDtypeStruct((t, d), BF16),
        compiler_params=_cp("parallel", "arbitrary"),
        name="merge",
    )(oz, cz, gates, gates, wa, wb, bias.reshape(1, d))


def _out_kernel(m_ref, w_ref, xp_ref, xt_ref, o_ref, *, prompt_blocks):
    acc = jnp.dot(m_ref[...], w_ref[...], preferred_element_type=F32)

    @pl.when(pl.program_id(0) < prompt_blocks)
    def _():
        o_ref[...] = xp_ref[...] + acc

    @pl.when(pl.program_id(0) >= prompt_blocks)
    def _():
        o_ref[...] = xt_ref[...] + acc


def _out_proj(merged, w_out, x_prompt, x_tail):
    t, k = merged.shape
    d = w_out.shape[1]
    nbp = x_prompt.shape[0] // TM
    return pl.pallas_call(
        functools.partial(_out_kernel, prompt_blocks=nbp),
        grid=(pl.cdiv(t, TM), d // TN),
        in_specs=[
            pl.BlockSpec((TM, k), lambda i, j: (i, 0)),
            pl.BlockSpec((k, TN), lambda i, j: (0, j)),
            pl.BlockSpec((TM, TN), lambda i, j: (jnp.minimum(i, nbp - 1), j)),
            pl.BlockSpec((TM, TN), lambda i, j: (jnp.maximum(i - nbp, 0), jnp.where(i >= nbp, j, 0))),
        ],
        out_specs=pl.BlockSpec((TM, TN), lambda i, j: (i, j)),
        out_shape=jax.ShapeDtypeStruct((t, d), F32),
        compiler_params=_cp("parallel", "arbitrary"),
        name="out_proj",
    )(merged, w_out, x_prompt, x_tail)


def _pack_bf16_pairs(x):
    half = x.shape[1] // 2
    bits = lax.bitcast_convert_type(x.astype(BF16).astype(F32), jnp.uint32)
    return (bits[:, :half] >> 16) | (bits[:, half:] & jnp.uint32(0xFFFF0000))


def _unpack_bf16_pairs(w):
    lo = lax.bitcast_convert_type(w << 16, F32).astype(BF16)
    hi = lax.bitcast_convert_type(w & jnp.uint32(0xFFFF0000), F32).astype(BF16)
    return lo, hi


def _router_kernel(h_ref, nw_ref, wr_ref, br_ref, xn_ref, ids_ref, wts_ref):
    x = h_ref[...]
    xn = x * lax.rsqrt(jnp.mean(x * x, axis=-1, keepdims=True) + EPS) * nw_ref[...]
    xn_ref[...] = _pack_bf16_pairs(xn)
    logits = _hdot(xn, wr_ref[...]) + br_ref[...]
    lane = lax.broadcasted_iota(jnp.int32, logits.shape, 1)
    neg = -jnp.inf
    big = jnp.int32(1 << 20)
    gl = jnp.where(lane < N_GROUPS, logits, neg)
    gmax = jnp.max(gl, axis=-1, keepdims=True)
    gsum = jnp.sum(jnp.exp(gl - gmax), axis=-1, keepdims=True)
    pg_top = 1.0 / gsum
    gidx = jnp.min(jnp.where(gl == gmax, lane, big), axis=-1, keepdims=True)
    lo = N_GROUPS + gidx * EXPERTS_PER_GROUP
    in_grp = (lane >= lo) & (lane < lo + EXPERTS_PER_GROUP)
    el = jnp.where(in_grp, logits, neg)
    emax = jnp.max(el, axis=-1, keepdims=True)
    ex = jnp.exp(el - emax)
    esum = jnp.sum(ex, axis=-1, keepdims=True)
    pe = ex / esum
    pe = jnp.where(in_grp, pe, -1.0)
    p1 = jnp.max(pe, axis=-1, keepdims=True)
    i1 = jnp.min(jnp.where(pe == p1, lane, big), axis=-1, keepdims=True)
    pe2 = jnp.where(lane == i1, -1.0, pe)
    p2 = jnp.max(pe2, axis=-1, keepdims=True)
    i2 = jnp.min(jnp.where(pe2 == p2, lane, big), axis=-1, keepdims=True)
    denom = p1 + p2
    w1 = pg_top * p1 / denom
    w2 = pg_top * p2 / denom
    ids_ref[...] = jnp.where(lane == 0, i1 - N_GROUPS, jnp.where(lane == 1, i2 - N_GROUPS, 0))
    wts_ref[...] = jnp.where(lane == 0, w1, jnp.where(lane == 1, w2, 0.0))


def _router(h1, norm_w, w_r, b_r, rows=256):
    t, d = h1.shape
    return pl.pallas_call(
        _router_kernel,
        grid=(pl.cdiv(t, rows),),
        in_specs=[
            pl.BlockSpec((rows, d), lambda i: (i, 0)),
            pl.BlockSpec((1, d), lambda i: (0, 0)),
            pl.BlockSpec((d, LANES), lambda i: (0, 0)),
            pl.BlockSpec((1, LANES), lambda i: (0, 0)),
        ],
        out_specs=[
            pl.BlockSpec((rows, d // 2), lambda i: (i, 0)),
            pl.BlockSpec((rows, LANES), lambda i: (i, 0)),
            pl.BlockSpec((rows, LANES), lambda i: (i, 0)),
        ],
        out_shape=[
            jax.ShapeDtypeStruct((t, d // 2), jnp.uint32),
            jax.ShapeDtypeStruct((t, LANES), jnp.int32),
            jax.ShapeDtypeStruct((t, LANES), F32),
        ],
        compiler_params=_cp("parallel"),
        name="router",
    )(h1, norm_w.reshape(1, d), w_r, b_r)


def _experts_kernel(ve_ref, nv_ref, vr_ref, vs_ref, slot_ref, xn_hbm, wg_ref, wu_ref, wd_ref, eo_hbm,
                    xbuf, xlo, xhi, hmid, wdb, acc_ref, gsem, ssem):
    v = pl.program_id(0)
    f = pl.program_id(1)
    nf = pl.num_programs(1)
    nv = nv_ref[0]
    half = xbuf.shape[1]

    def gather_copy(visit, r):
        tok = slot_ref[vs_ref[visit] + r] >> 1
        return pltpu.make_async_copy(xn_hbm.at[pl.ds(tok, 1), :], xbuf.at[pl.ds(r, 1), :], gsem)

    def scatter_copy(visit, r):
        slot = slot_ref[vs_ref[visit] + r]
        return pltpu.make_async_copy(acc_ref.at[pl.ds(r, 1), :], eo_hbm.at[slot & 1, pl.ds(slot >> 1, 1), :], ssem)

    gather_rows_wait = lambda: pltpu.make_async_copy(
        xn_hbm.at[pl.ds(0, DMA_UNROLL), :], xbuf.at[pl.ds(0, DMA_UNROLL), :], gsem).wait()
    scatter_rows_wait = lambda: pltpu.make_async_copy(
        acc_ref.at[pl.ds(0, DMA_UNROLL), :], eo_hbm.at[0, pl.ds(0, DMA_UNROLL), :], ssem).wait()

    def loop(lo, hi, fn):
        def body(i, carry):
            fn(i)
            return carry
        lax.fori_loop(lo, hi, body, 0)

    def unrolled(fn):
        def run(i):
            for j in range(DMA_UNROLL):
                fn(i * DMA_UNROLL + j)
        return run

    def start_gather(visit):
        loop(0, (vr_ref[visit] + DMA_UNROLL - 1) // DMA_UNROLL, unrolled(lambda r: gather_copy(visit, r).start()))

    def wait_gather(visit):
        loop(0, (vr_ref[visit] + DMA_UNROLL - 1) // DMA_UNROLL, lambda i: gather_rows_wait())

    def start_scatter(visit):
        n = vr_ref[visit]
        loop(0, n // DMA_UNROLL, unrolled(lambda r: scatter_copy(visit, r).start()))
        loop((n // DMA_UNROLL) * DMA_UNROLL, n, lambda r: scatter_copy(visit, r).start())

    def wait_scatter(visit):
        n = vr_ref[visit]
        loop(0, n // DMA_UNROLL, lambda i: scatter_rows_wait())
        loop((n // DMA_UNROLL) * DMA_UNROLL, n, lambda r: scatter_copy(visit, r).wait())

    @pl.when((v == 0) & (f == 0))
    def _():
        xbuf[...] = jnp.zeros_like(xbuf)
        start_gather(0)

    @pl.when((v < nv) & (f == 0))
    def _():
        wait_gather(v)
        lo, hi = _unpack_bf16_pairs(xbuf[...])
        xlo[...] = lo
        xhi[...] = hi

        @pl.when(v + 1 < nv)
        def _():
            start_gather(v + 1)

    @pl.when(v < nv)
    def _():
        wg = wg_ref[0].astype(BF16)
        wu = wu_ref[0].astype(BF16)
        xa = xlo[...]
        xb = xhi[...]
        g = jnp.dot(xa, wg[:half], preferred_element_type=F32) + jnp.dot(xb, wg[half:], preferred_element_type=F32)
        u = jnp.dot(xa, wu[:half], preferred_element_type=F32) + jnp.dot(xb, wu[half:], preferred_element_type=F32)
        hmid[f] = (_silu(g) * u).astype(BF16)
        wdb[f] = wd_ref[0].astype(BF16)

        @pl.when(f == nf - 1)
        def _():
            @pl.when(v > 0)
            def _():
                wait_scatter(v - 1)
            out = jnp.dot(hmid[0], wdb[0], preferred_element_type=F32)
            for j in range(1, hmid.shape[0]):
                out = out + jnp.dot(hmid[j], wdb[j], preferred_element_type=F32)
            acc_ref[...] = out
            start_scatter(v)

            @pl.when(v == nv - 1)
            def _():
                wait_scatter(v)


def _experts(visit_expert, n_visits, visit_rows, visit_start, row_slot, xn_packed, w_gate, w_up, w_down, max_visits):
    t, half = xn_packed.shape
    d = 2 * half
    fdim = w_gate.shape[2]
    nf = fdim // EXPERT_F_TILE
    ftile = lambda v, f, nv: jnp.where(v < nv[0], f, nf - 1)
    grid_spec = pltpu.PrefetchScalarGridSpec(
        num_scalar_prefetch=5,
        grid=(max_visits, nf),
        in_specs=[
            pl.BlockSpec(memory_space=pl.ANY),
            pl.BlockSpec((1, d, EXPERT_F_TILE), lambda v, f, ve, nv, vr, vs, sl: (ve[v], 0, ftile(v, f, nv))),
            pl.BlockSpec((1, d, EXPERT_F_TILE), lambda v, f, ve, nv, vr, vs, sl: (ve[v], 0, ftile(v, f, nv))),
            pl.BlockSpec((1, EXPERT_F_TILE, d), lambda v, f, ve, nv, vr, vs, sl: (ve[v], ftile(v, f, nv), 0)),
        ],
        out_specs=pl.BlockSpec(memory_space=pl.ANY),
        scratch_shapes=[
            pltpu.VMEM((GROUP_ROWS, half), jnp.uint32),
            pltpu.VMEM((GROUP_ROWS, half), BF16),
            pltpu.VMEM((GROUP_ROWS, half), BF16),
            pltpu.VMEM((nf, GROUP_ROWS, EXPERT_F_TILE), BF16),
            pltpu.VMEM((nf, EXPERT_F_TILE, d), BF16),
            pltpu.VMEM((GROUP_ROWS, d), F32),
            pltpu.SemaphoreType.DMA(()),
            pltpu.SemaphoreType.DMA(()),
        ],
    )
    return pl.pallas_call(
        _experts_kernel,
        grid_spec=grid_spec,
        out_shape=jax.ShapeDtypeStruct((2, t, d), F32),
        compiler_params=_cp("arbitrary", "arbitrary"),
        name="experts",
    )(visit_expert, n_visits, visit_rows, visit_start, row_slot, xn_packed, w_gate, w_up, w_down)


def _final_kernel(h_ref, o1_ref, o2_ref, wts_ref, nw_ref, y_ref):
    wts = wts_ref[...]
    h = h_ref[...] + wts[:, 0:1] * o1_ref[0] + wts[:, 1:2] * o2_ref[0]
    y_ref[...] = h * lax.rsqrt(jnp.mean(h * h, axis=-1, keepdims=True) + EPS) * nw_ref[...]


def _final(h1, eo, wts, norm_w, n_rows, row_block0, rows=FINAL_ROWS):
    d = h1.shape[1]
    rb = lambda i: (i + row_block0, 0)
    return pl.pallas_call(
        _final_kernel,
        grid=(n_rows // rows,),
        in_specs=[
            pl.BlockSpec((rows, d), rb),
            pl.BlockSpec((1, rows, d), lambda i: (0, i + row_block0, 0)),
            pl.BlockSpec((1, rows, d), lambda i: (1, i + row_block0, 0)),
            pl.BlockSpec((rows, LANES), rb),
            pl.BlockSpec((1, d), lambda i: (0, 0)),
        ],
        out_specs=pl.BlockSpec((rows, d), lambda i: (i, 0)),
        out_shape=jax.ShapeDtypeStruct((n_rows, d), F32),
        compiler_params=_cp("parallel"),
        name="final",
    )(h1, eo, eo, wts, norm_w.reshape(1, d))


def _dispatch_plan(ids, max_visits):
    flat_e = ids[:, :2].reshape(-1)
    order = jnp.argsort(flat_e, stable=True).astype(jnp.int32)
    counts = jnp.sum((flat_e[:, None] == jnp.arange(N_EXPERTS, dtype=jnp.int32)[None, :]).astype(jnp.int32), axis=0)
    tiles = (counts + GROUP_ROWS - 1) // GROUP_ROWS
    tile_end = jnp.cumsum(tiles)
    tile_start = tile_end - tiles
    group_start = jnp.cumsum(counts) - counts
    n_visits = tile_end[-1]
    visit = jnp.arange(max_visits, dtype=jnp.int32)
    visit_expert = jnp.sum((tile_end[None, :] <= jnp.minimum(visit, n_visits - 1)[:, None]).astype(jnp.int32), axis=1)
    visit_expert = jnp.minimum(visit_expert, N_EXPERTS - 1)
    tile_in_expert = visit - tile_start[visit_expert]
    rows_left = counts[visit_expert] - GROUP_ROWS * tile_in_expert
    valid = visit < n_visits
    visit_rows = jnp.where(valid, jnp.clip(rows_left, 0, GROUP_ROWS), 0).astype(jnp.int32)
    visit_start = jnp.where(valid, group_start[visit_expert] + GROUP_ROWS * tile_in_expert, 0).astype(jnp.int32)
    row_slot = jnp.pad(order, (0, DMA_UNROLL))
    return row_slot, visit_expert.astype(jnp.int32), n_visits.reshape(1).astype(jnp.int32), visit_rows, visit_start


def kernel(x_prompt, x_sample, state_delta, state_conv_qkv, state_conv_b, meta_tokens,
           norm1, w_in, w_conv_qkv, a_log, dt_bias, w_onorm, w_proj_a, w_dw, b_dw, ln_w, ln_b,
           w_proj_b, b_proj_b, w_out, norm2, w_rg, b_rg, w_re, b_re, w_e_gate, w_e_up, w_e_down,
           final_norm):
    bsz, seq, d = x_prompt.shape
    nsmp, steps, _ = x_sample.shape
    n_meta = meta_tokens.shape[0]
    depth = norm1.shape[0]
    assert depth == 1
    tp = bsz * seq
    ts = nsmp * steps
    t_all = tp + ts + n_meta
    qkv_w = 3 * DN_WIDTH
    conf_w = w_dw.shape[-1]
    hist = CONF_KERNEL - 1

    x_p = x_prompt.reshape(tp, d)
    tail_rows = pl.cdiv(ts + n_meta, TM) * TM
    x_tail = jnp.concatenate([x_sample.transpose(1, 0, 2).reshape(ts, d), meta_tokens,
                              jnp.zeros((tail_rows - ts - n_meta, d), F32)], axis=0)

    wi = w_in[0]
    o_z = qkv_w + DN_WIDTH
    o_glu = o_z + 2 * N_HEADS
    o_gate = o_glu + 2 * conf_w
    w_qkvz = wi[:, :o_z].astype(BF16)
    w_ab = jnp.pad(wi[:, o_z:o_glu], ((0, 0), (0, LANES - 2 * N_HEADS))).astype(BF16)
    w_glu = wi[:, o_glu:o_gate].astype(BF16)
    w_gate = wi[:, o_gate:].astype(BF16)
    alog_pad = jnp.pad(a_log[0], (0, LANES - N_HEADS)).reshape(1, LANES)
    dtb_pad = jnp.pad(dt_bias[0], (0, LANES - N_HEADS)).reshape(1, LANES)

    hn = _rmsnorm_bf16(x_p, x_tail, norm1[0], t_all)
    qkvz = _proj(_mm_plain_kernel, hn, w_qkvz, o_z, F32, "proj_qkvz")
    gbeta = _proj_decay(hn, w_ab, alog_pad, dtb_pad)
    glu = _proj_glu(hn, w_glu)
    gates = _proj(_mm_sigmoid_kernel, hn, w_gate, 2 * d, BF16, "proj_gates")

    wc = w_conv_qkv[0]
    meta_blk = (tp + ts) // n_meta
    qkv_meta = _prep_long(qkvz, wc, n_meta, n_meta, meta_blk, lambda i: 0, True, "prep_meta")
    rows_p = 256
    bps = seq // rows_p
    meta_halo = (tp + ts + n_meta) // 8 - 1
    qkv_p = _prep_long(qkvz, wc, tp, rows_p, 0,
                       lambda i: jnp.where(i % bps == 0, meta_halo, i * (rows_p // 8) - 1), False, "prep_prompt")
    raw_s = qkvz[tp:tp + ts, :qkv_w].reshape(steps, nsmp, qkv_w)
    st_qkv_tm = state_conv_qkv[0].transpose(1, 0, 2)
    qkv_s_tm = _prep_short(raw_s, st_qkv_tm, wc)

    zero_state = jnp.zeros((1, N_HEADS, HEAD_DIM, HEAD_DIM), F32)
    oz_meta, s_meta = _delta(qkv_meta, gbeta, qkvz, 3, zero_state, w_onorm[0], 1, 1, n_meta,
                             lambda n, ci: 0, lambda n, ci: meta_blk, lambda n: 0, "delta_meta")
    n_chunks = seq // CHUNK
    prompt_blk = lambda n, ci: n * n_chunks + ci
    oz_p, s_p = _delta(qkv_p, gbeta, qkvz, 3, s_meta, w_onorm[0], bsz, n_chunks, CHUNK,
                       prompt_blk, prompt_blk, lambda n: 0, "delta_prompt")

    cpad = 8
    to_bm = lambda a: jnp.pad(a.transpose(1, 0, 2), ((0, 0), (0, cpad - steps), (0, 0))).reshape(nsmp * cpad, a.shape[-1])
    qkv_s = to_bm(qkv_s_tm)
    gb_s = to_bm(gbeta[tp:tp + ts].reshape(steps, nsmp, LANES))
    z_s = to_bm(qkvz[tp:tp + ts, qkv_w:].reshape(steps, nsmp, DN_WIDTH))
    oz_s_bm, s_s = _delta(qkv_s, gb_s, z_s, 0, state_delta[0], w_onorm[0], nsmp, 1, cpad,
                          lambda n, ci: n, lambda n, ci: n, lambda n: n, "delta_sample")
    oz_s = oz_s_bm.reshape(nsmp, cpad, DN_WIDTH)[:, :steps].transpose(1, 0, 2).reshape(ts, DN_WIDTH)
    oz = jnp.concatenate([oz_p, oz_s, oz_meta], axis=0)

    zeros32 = jnp.zeros((32, conf_w), F32)
    cz_meta = _conf_long(glu, zeros32, w_dw[0], b_dw[0], ln_w[0], ln_b[0], n_meta, n_meta, meta_blk, 1,
                         lambda i: 0, "conf_meta")
    init_p = jnp.concatenate([jnp.zeros((32 - n_meta, conf_w), F32), glu[tp + ts:]], axis=0)
    rows_c = 128
    cz_p = _conf_long(glu, init_p, w_dw[0], b_dw[0], ln_w[0], ln_b[0], tp, rows_c, 0, seq // rows_c,
                      lambda i: jnp.maximum(i * (rows_c // 32) - 1, 0), "conf_prompt")
    glu_s_tm = glu[tp:tp + ts].reshape(steps, nsmp, conf_w)
    hist_s_tm = state_conv_b[0].transpose(1, 0, 2)
    cz_s = _conf_short(glu_s_tm, hist_s_tm, w_dw[0], b_dw[0], ln_w[0], ln_b[0]).reshape(ts, conf_w)
    cz = jnp.concatenate([cz_p, cz_s, cz_meta], axis=0)

    merged = _merge(oz, cz, gates, w_proj_a[0].astype(BF16), w_proj_b[0].astype(BF16), b_proj_b[0])
    h1 = _out_proj(merged, w_out[0].astype(BF16), x_p, x_tail)

    w_r = jnp.pad(jnp.concatenate([w_rg[0], w_re[0]], axis=1), ((0, 0), (0, LANES - N_GROUPS - N_EXPERTS)))
    b_r = jnp.pad(jnp.concatenate([b_rg[0], b_re[0]]), (0, LANES - N_GROUPS - N_EXPERTS)).reshape(1, LANES)
    xn2, ids, wts = _router(h1, norm2[0], w_r, b_r)
    max_visits = N_EXPERTS + (2 * t_all) // GROUP_ROWS
    row_slot, visit_expert, n_visits, visit_rows, visit_start = _dispatch_plan(ids, max_visits)
    eo = _experts(visit_expert, n_visits, visit_rows, visit_start, row_slot, xn2,
                  w_e_gate[0], w_e_up[0], w_e_down[0], max_visits)
    y_p = _final(h1, eo, wts, final_norm, tp, 0)
    y_s = _final(h1, eo, wts, final_norm, ts, tp // FINAL_ROWS)

    y_prompt = y_p.reshape(bsz, seq, d)
    y_sample = y_s.reshape(steps, nsmp, d).transpose(1, 0, 2)
    new_cq_p = jnp.stack([qkvz[(b + 1) * seq - (SHORT_CONV - 1):(b + 1) * seq, :qkv_w] for b in range(bsz)])
    new_cb_p = jnp.stack([glu[(b + 1) * seq - hist:(b + 1) * seq] for b in range(bsz)])
    cq_s = jnp.concatenate([st_qkv_tm, raw_s], axis=0)[-(SHORT_CONV - 1):].transpose(1, 0, 2)
    cb_s = jnp.concatenate([hist_s_tm, glu_s_tm], axis=0)[-hist:].transpose(1, 0, 2)
    return (y_prompt, y_sample, s_p[None], new_cq_p[None], new_cb_p[None], s_s[None], cq_s[None], cb_s[None])
```

```python
import functools

import jax
import jax.numpy as jnp
from jax import lax
from jax.experimental import pallas as pl
from jax.experimental.pallas import tpu as pltpu

F32 = jnp.float32
BF16 = jnp.bfloat16
EPS = 1e-6

N_HEADS = 16
HEAD_DIM = 128
DN_WIDTH = N_HEADS * HEAD_DIM
SHORT_CONV = 4
CONF_KERNEL = 31
N_GROUPS = 8
EXPERTS_PER_GROUP = 8
N_EXPERTS = N_GROUPS * EXPERTS_PER_GROUP
CHUNK = 64
LANES = 128
HIGHEST = lax.Precision.HIGHEST

TM = 1024
TN = 512
GROUP_ROWS = 320
EXPERT_F_TILE = 256
DMA_UNROLL = 8
ROW_DMA_PRIORITY = 1
CONF_LANE_CHUNK = 512
FINAL_ROWS = 256
VMEM_LIMIT = 56 * 1024 * 1024


def _cp(*sem):
    return pltpu.CompilerParams(dimension_semantics=sem, vmem_limit_bytes=VMEM_LIMIT)


def _sigmoid(x):
    return 1.0 / (1.0 + jnp.exp(-x))


def _silu(x):
    return x * _sigmoid(x)


def _softplus(x):
    return jnp.maximum(x, 0.0) + jnp.log1p(jnp.exp(-jnp.abs(x)))


def _rmsnorm_kernel(xp_ref, xt_ref, w_ref, o_ref, *, prompt_blocks):
    def norm(x):
        ms = jnp.mean(x * x, axis=-1, keepdims=True)
        return (x * lax.rsqrt(ms + EPS) * w_ref[...]).astype(o_ref.dtype)

    @pl.when(pl.program_id(0) < prompt_blocks)
    def _():
        o_ref[...] = norm(xp_ref[...])

    @pl.when(pl.program_id(0) >= prompt_blocks)
    def _():
        o_ref[...] = norm(xt_ref[...])


def _rmsnorm_bf16(x_prompt, x_tail, w, t_all, rows=256):
    tp, d = x_prompt.shape
    nbp = tp // rows
    return pl.pallas_call(
        functools.partial(_rmsnorm_kernel, prompt_blocks=nbp),
        grid=(pl.cdiv(t_all, rows),),
        in_specs=[
            pl.BlockSpec((rows, d), lambda i: (jnp.minimum(i, nbp - 1), 0)),
            pl.BlockSpec((rows, d), lambda i: (jnp.maximum(i - nbp, 0), 0)),
            pl.BlockSpec((1, d), lambda i: (0, 0)),
        ],
        out_specs=pl.BlockSpec((rows, d), lambda i: (i, 0)),
        out_shape=jax.ShapeDtypeStruct((t_all, d), BF16),
        compiler_params=_cp("parallel"),
        name="rmsnorm1",
    )(x_prompt, x_tail, w.reshape(1, d))


def _mm_plain_kernel(x_ref, w_ref, o_ref):
    o_ref[...] = jnp.dot(x_ref[...], w_ref[...], preferred_element_type=F32).astype(o_ref.dtype)


def _mm_sigmoid_kernel(x_ref, w_ref, o_ref):
    o_ref[...] = _sigmoid(jnp.dot(x_ref[...], w_ref[...], preferred_element_type=F32)).astype(o_ref.dtype)


def _mm_glu_kernel(x_ref, wu_ref, wg_ref, o_ref):
    x = x_ref[...]
    u = jnp.dot(x, wu_ref[...], preferred_element_type=F32)
    g = jnp.dot(x, wg_ref[...], preferred_element_type=F32)
    o_ref[...] = u * _sigmoid(g)


def _mm_decay_kernel(x_ref, w_ref, alog_ref, dtb_ref, o_ref):
    acc = jnp.dot(x_ref[...], w_ref[...], preferred_element_type=F32)
    lane = lax.broadcasted_iota(jnp.int32, acc.shape, 1)
    g = -jnp.exp(alog_ref[...]) * _softplus(acc + dtb_ref[...])
    beta = _sigmoid(acc)
    o_ref[...] = jnp.where(lane < N_HEADS, g, jnp.where(lane < 2 * N_HEADS, beta, 0.0))


def _proj(kernel_fn, hn, w, n_out, out_dtype, name, tn=TN):
    t, k = hn.shape
    return pl.pallas_call(
        kernel_fn,
        grid=(pl.cdiv(t, TM), n_out // tn),
        in_specs=[pl.BlockSpec((TM, k), lambda i, j: (i, 0)), pl.BlockSpec((k, tn), lambda i, j: (0, j))],
        out_specs=pl.BlockSpec((TM, tn), lambda i, j: (i, j)),
        out_shape=jax.ShapeDtypeStruct((t, n_out), out_dtype),
        compiler_params=_cp("parallel", "arbitrary"),
        name=name,
    )(hn, w)


def _proj_glu(hn, w_glu):
    t, k = hn.shape
    half = w_glu.shape[1] // 2
    nb = half // TN
    return pl.pallas_call(
        _mm_glu_kernel,
        grid=(pl.cdiv(t, TM), nb),
        in_specs=[
            pl.BlockSpec((TM, k), lambda i, j: (i, 0)),
            pl.BlockSpec((k, TN), lambda i, j: (0, j)),
            pl.BlockSpec((k, TN), lambda i, j: (0, j + nb)),
        ],
        out_specs=pl.BlockSpec((TM, TN), lambda i, j: (i, j)),
        out_shape=jax.ShapeDtypeStruct((t, half), F32),
        compiler_params=_cp("parallel", "arbitrary"),
        name="proj_glu",
    )(hn, w_glu, w_glu)


def _proj_decay(hn, w_ab, alog_pad, dtb_pad):
    t, k = hn.shape
    return pl.pallas_call(
        _mm_decay_kernel,
        grid=(pl.cdiv(t, TM),),
        in_specs=[
            pl.BlockSpec((TM, k), lambda i: (i, 0)),
            pl.BlockSpec((k, LANES), lambda i: (0, 0)),
            pl.BlockSpec((1, LANES), lambda i: (0, 0)),
            pl.BlockSpec((1, LANES), lambda i: (0, 0)),
        ],
        out_specs=pl.BlockSpec((TM, LANES), lambda i: (i, 0)),
        out_shape=jax.ShapeDtypeStruct((t, LANES), F32),
        compiler_params=_cp("parallel"),
        name="proj_decay",
    )(hn, w_ab, alog_pad, dtb_pad)


def _head_normalize(y, sec):
    outs = []
    for h in range(N_HEADS):
        yh = y[:, h * HEAD_DIM:(h + 1) * HEAD_DIM]
        inv = lax.rsqrt(jnp.sum(yh * yh, axis=-1, keepdims=True) + EPS)
        scale = jnp.where(sec == 0, inv * (HEAD_DIM ** -0.5), jnp.where(sec == 1, inv, 1.0))
        outs.append(yh * scale)
    return jnp.concatenate(outs, axis=-1)


def _prep_long_kernel(cur_ref, halo_ref, w_ref, o_ref, ext_ref, *, rows, zero_halo):
    sec = pl.program_id(1)
    halo = halo_ref[...]
    ext_ref[0:8, :] = jnp.zeros_like(halo) if zero_halo else halo
    ext_ref[8:, :] = cur_ref[...]
    acc = w_ref[SHORT_CONV - 1:SHORT_CONV, :] * cur_ref[...]
    for s in range(1, SHORT_CONV):
        acc = acc + w_ref[SHORT_CONV - 1 - s:SHORT_CONV - s, :] * ext_ref[pl.ds(8 - s, rows), :]
    o_ref[...] = _head_normalize(_silu(acc), sec)


def _prep_long(qkvz, w_conv, n_rows, rows, row_block0, halo_index, zero_halo, name):
    nblk = n_rows // rows
    kern = functools.partial(_prep_long_kernel, rows=rows, zero_halo=zero_halo)
    return pl.pallas_call(
        kern,
        grid=(nblk, 3),
        in_specs=[
            pl.BlockSpec((rows, DN_WIDTH), lambda i, s: (i + row_block0, s)),
            pl.BlockSpec((8, DN_WIDTH), lambda i, s: (halo_index(i), s)),
            pl.BlockSpec((SHORT_CONV, DN_WIDTH), lambda i, s: (0, s)),
        ],
        out_specs=pl.BlockSpec((rows, DN_WIDTH), lambda i, s: (i, s)),
        out_shape=jax.ShapeDtypeStruct((n_rows, 3 * DN_WIDTH), F32),
        scratch_shapes=[pltpu.VMEM((rows + 8, DN_WIDTH), F32)],
        compiler_params=_cp("parallel", "arbitrary"),
        name=name,
    )(qkvz, qkvz, w_conv)


def _prep_short_kernel(x_ref, st_ref, w_ref, o_ref, *, steps):
    sec = pl.program_id(0)
    buf = [st_ref[i] for i in range(SHORT_CONV - 1)] + [x_ref[t] for t in range(steps)]
    for t in range(steps):
        acc = w_ref[0:1, :] * buf[t]
        for i in range(1, SHORT_CONV):
            acc = acc + w_ref[i:i + 1, :] * buf[t + i]
        o_ref[t] = _head_normalize(_silu(acc), sec)


def _prep_short(x_tm, state_tm, w_conv):
    steps, n, _ = x_tm.shape
    kern = functools.partial(_prep_short_kernel, steps=steps)
    return pl.pallas_call(
        kern,
        grid=(3,),
        in_specs=[
            pl.BlockSpec((steps, n, DN_WIDTH), lambda s: (0, 0, s)),
            pl.BlockSpec((SHORT_CONV - 1, n, DN_WIDTH), lambda s: (0, 0, s)),
            pl.BlockSpec((SHORT_CONV, DN_WIDTH), lambda s: (0, s)),
        ],
        out_specs=pl.BlockSpec((steps, n, DN_WIDTH), lambda s: (0, 0, s)),
        out_shape=jax.ShapeDtypeStruct((steps, n, 3 * DN_WIDTH), F32),
        compiler_params=_cp("arbitrary"),
        name="prep_sample",
    )(x_tm, state_tm, w_conv)


def _hdot(a, b):
    return jnp.dot(a, b, preferred_element_type=F32, precision=HIGHEST)


def _split_bf16(a):
    hi = a.astype(BF16)
    lo = (a - hi.astype(F32)).astype(BF16)
    return hi, lo


def _bmm(a, b):
    return jnp.einsum("hik,hkj->hij", a.astype(BF16), b.astype(BF16), preferred_element_type=F32)


def _bmm3(a, b):
    ah, al = _split_bf16(a)
    bh, bl = _split_bf16(b)
    f = lambda x, y: jnp.einsum("hik,hkj->hij", x, y, preferred_element_type=F32)
    return f(ah, bh) + (f(ah, bl) + f(al, bh))


def _unit_lower_inverse(low, c):
    base = min(c, 16)
    row = lax.broadcasted_iota(jnp.int32, (c, c), 0)
    col = lax.broadcasted_iota(jnp.int32, (c, c), 1)
    eye = (row == col).astype(F32)[None]
    diag = jnp.where(((row // base) == (col // base))[None], low, 0.0) if c > base else low
    inv = eye - diag
    power = diag
    k = 2
    while k < base:
        power = _bmm3(power, power)
        inv = inv + _bmm3(inv, power)
        k *= 2
    blk = base
    while blk < c:
        sel = ((row // (2 * blk)) == (col // (2 * blk))) & (((row // blk) % 2) == 1) & (((col // blk) % 2) == 0)
        off = jnp.where(sel[None], low, 0.0)
        inv = inv - _bmm3(inv, _bmm3(off, inv))
        blk *= 2
    return inv


def _delta_kernel(q_ref, k_ref, v_ref, gb_ref, z_ref, s0_ref, wn_ref, o_ref, sout_ref, s_scr, *, c, heads_per_group):
    ci = pl.program_id(1)

    @pl.when(ci == 0)
    def _():
        s_scr[...] = s0_ref[0]

    row = lax.broadcasted_iota(jnp.int32, (c, c), 0)
    col = lax.broadcasted_iota(jnp.int32, (c, c), 1)
    causal = (row >= col)[None]
    strict = (row > col)[None]
    gb = gb_ref[...]
    gcum = _hdot((row >= col).astype(F32), gb)
    gcum_t = gcum.T
    wn = wn_ref[...]
    d = HEAD_DIM
    for h0 in range(0, N_HEADS, heads_per_group):
        heads = range(h0, h0 + heads_per_group)
        hs = slice(h0, h0 + heads_per_group)
        heads_of = lambda ref: jnp.stack([ref[:, h * d:(h + 1) * d] for h in heads])
        q = heads_of(q_ref)
        k = heads_of(k_ref)
        v = heads_of(v_ref)
        g_col = jnp.stack([gcum[:, h:h + 1] for h in heads])
        g_row = jnp.stack([gcum_t[h:h + 1, :] for h in heads])
        beta = jnp.stack([gb[:, N_HEADS + h:N_HEADS + h + 1] for h in heads])
        g_last = g_col[:, c - 1:c, :]
        decay = jnp.where(causal, jnp.exp(g_col - g_row), 0.0)
        exp_g = jnp.exp(g_col)
        kb = k * beta
        kk = jnp.einsum("hid,hjd->hij", jnp.concatenate([kb, q], axis=1).astype(BF16), k.astype(BF16),
                        preferred_element_type=F32)
        lower = jnp.where(strict, kk[:, :c] * decay, 0.0)
        attn = kk[:, c:] * decay
        tinv = _unit_lower_inverse(lower, c)
        rhs = jnp.concatenate([v * beta, kb * exp_g], axis=-1)
        sol = _bmm3(tinv, rhs)
        s = s_scr[hs]
        both = _bmm(jnp.concatenate([sol[:, :, d:], q * exp_g], axis=1), s)
        u = sol[:, :, :d] - both[:, :c]
        o = both[:, c:] + _bmm(attn, u)
        kd = k * jnp.exp(g_last - g_col)
        upd = lax.dot_general(kd.astype(BF16), u.astype(BF16), (((1,), (1,)), ((0,), (0,))),
                              preferred_element_type=F32)
        s_scr[hs] = s * jnp.exp(g_last) + upd
        on = o * lax.rsqrt(jnp.mean(o * o, axis=-1, keepdims=True) + EPS) * wn
        out = (on * _silu(heads_of(z_ref))).astype(o_ref.dtype)
        for i, h in enumerate(heads):
            o_ref[:, h * d:(h + 1) * d] = out[i]

    @pl.when(ci == pl.num_programs(1) - 1)
    def _():
        sout_ref[0] = s_scr[...]


def _delta(qkv, gb, zsrc, z_col_block, s0, w_onorm, n_seq, n_chunks, c, row_block, aux_block, s0_index, name,
           heads_per_group=N_HEADS):
    kern = functools.partial(_delta_kernel, c=c, heads_per_group=heads_per_group)
    rows_out = n_seq * n_chunks * c
    return pl.pallas_call(
        kern,
        grid=(n_seq, n_chunks),
        in_specs=[
            pl.BlockSpec((c, DN_WIDTH), lambda n, ci: (row_block(n, ci), 0)),
            pl.BlockSpec((c, DN_WIDTH), lambda n, ci: (row_block(n, ci), 1)),
            pl.BlockSpec((c, DN_WIDTH), lambda n, ci: (row_block(n, ci), 2)),
            pl.BlockSpec((c, LANES), lambda n, ci: (aux_block(n, ci), 0)),
            pl.BlockSpec((c, DN_WIDTH), lambda n, ci: (aux_block(n, ci), z_col_block)),
            pl.BlockSpec((1, N_HEADS, HEAD_DIM, HEAD_DIM), lambda n, ci: (s0_index(n), 0, 0, 0)),
            pl.BlockSpec((1, HEAD_DIM), lambda n, ci: (0, 0)),
        ],
        out_specs=[
            pl.BlockSpec((c, DN_WIDTH), lambda n, ci: (n * n_chunks + ci, 0)),
            pl.BlockSpec((1, N_HEADS, HEAD_DIM, HEAD_DIM), lambda n, ci: (n, 0, 0, 0)),
        ],
        out_shape=[
            jax.ShapeDtypeStruct((rows_out, DN_WIDTH), BF16),
            jax.ShapeDtypeStruct((n_seq, N_HEADS, HEAD_DIM, HEAD_DIM), F32),
        ],
        scratch_shapes=[pltpu.VMEM((N_HEADS, HEAD_DIM, HEAD_DIM), F32)],
        compiler_params=_cp("parallel", "arbitrary"),
        name=name,
    )(qkv, qkv, qkv, gb, zsrc, s0, w_onorm.reshape(1, HEAD_DIM))


def _layernorm_silu(c, lnw, lnb):
    mu = jnp.mean(c, axis=-1, keepdims=True)
    xc = c - mu
    y = xc * lax.rsqrt(jnp.mean(xc * xc, axis=-1, keepdims=True) + EPS)
    return _silu(y * lnw + lnb)


def _conf_long_kernel(cur_ref, prev_ref, init_ref, w_ref, b_ref, lnw_ref, lnb_ref, o_ref, ext_ref, c_ref, *, rows, blocks_per_seq):
    hist = CONF_KERNEL - 1
    first = (pl.program_id(0) % blocks_per_seq) == 0
    ext_ref[0:32, :] = jnp.where(first, init_ref[...], prev_ref[...])
    ext_ref[32:, :] = cur_ref[...]
    cw = cur_ref.shape[1]
    lc = min(CONF_LANE_CHUNK, cw)
    rowid = lax.broadcasted_iota(jnp.int32, (8, lc), 0)
    for c0 in range(0, cw, lc):
        cs = slice(c0, c0 + lc)

        def qtile(n, r):
            acc = None
            for a in range(5):
                i = 8 * a + r - 2
                if 0 <= i <= hist:
                    term = w_ref[8 * i:8 * i + 8, cs] * ext_ref[8 * (n + a):8 * (n + a) + 8, cs]
                    acc = term if acc is None else acc + term
            return acc

        held = [None] + [qtile(0, r) for r in range(1, 8)]
        for m in range(rows // 8):
            out = qtile(m, 0)
            for r in range(1, 8):
                nxt = qtile(m + 1, r)
                out = out + pltpu.roll(jnp.where(rowid >= r, held[r], nxt), 8 - r, axis=0)
                held[r] = nxt
            c_ref[8 * m:8 * m + 8, cs] = out
    o_ref[...] = _layernorm_silu(c_ref[...] + b_ref[...], lnw_ref[...], lnb_ref[...]).astype(o_ref.dtype)


def _conf_long(glu, init_hist, w_dw, b_dw, ln_w, ln_b, n_rows, rows, row_block0, blocks_per_seq, prev_index, name):
    cw = glu.shape[1]
    nblk = n_rows // rows
    kern = functools.partial(_conf_long_kernel, rows=rows, blocks_per_seq=blocks_per_seq)
    vec = lambda a: a.reshape(1, cw)
    return pl.pallas_call(
        kern,
        grid=(nblk,),
        in_specs=[
            pl.BlockSpec((rows, cw), lambda i: (i + row_block0, 0)),
            pl.BlockSpec((32, cw), lambda i: (prev_index(i), 0)),
            pl.BlockSpec((32, cw), lambda i: (0, 0)),
            pl.BlockSpec((8 * CONF_KERNEL, cw), lambda i: (0, 0)),
            pl.BlockSpec((1, cw), lambda i: (0, 0)),
            pl.BlockSpec((1, cw), lambda i: (0, 0)),
            pl.BlockSpec((1, cw), lambda i: (0, 0)),
        ],
        out_specs=pl.BlockSpec((rows, cw), lambda i: (i, 0)),
        out_shape=jax.ShapeDtypeStruct((n_rows, cw), BF16),
        scratch_shapes=[pltpu.VMEM((rows + 32, cw), F32), pltpu.VMEM((rows, cw), F32)],
        compiler_params=_cp("parallel"),
        name=name,
    )(glu, glu, init_hist, jnp.repeat(w_dw, 8, axis=0), vec(b_dw), vec(ln_w), vec(ln_b))


def _conf_short_kernel(x_ref, hist_ref, w_ref, b_ref, lnw_ref, lnb_ref, o_ref, *, steps):
    nh = CONF_KERNEL - 1
    for t in range(steps):
        acc = jnp.zeros(x_ref.shape[1:], F32)
        for i in range(CONF_KERNEL):
            j = t + i
            src = hist_ref[j] if j < nh else x_ref[j - nh]
            acc = acc + w_ref[i:i + 1, :] * src
        o_ref[t] = _layernorm_silu(acc + b_ref[...], lnw_ref[...], lnb_ref[...]).astype(o_ref.dtype)


def _conf_short(x_tm, hist_tm, w_dw, b_dw, ln_w, ln_b, nb=32):
    steps, n, cw = x_tm.shape
    kern = functools.partial(_conf_short_kernel, steps=steps)
    vec = lambda a: a.reshape(1, cw)
    return pl.pallas_call(
        kern,
        grid=(n // nb,),
        in_specs=[
            pl.BlockSpec((steps, nb, cw), lambda i: (0, i, 0)),
            pl.BlockSpec((CONF_KERNEL - 1, nb, cw), lambda i: (0, i, 0)),
            pl.BlockSpec((CONF_KERNEL, cw), lambda i: (0, 0)),
            pl.BlockSpec((1, cw), lambda i: (0, 0)),
            pl.BlockSpec((1, cw), lambda i: (0, 0)),
            pl.BlockSpec((1, cw), lambda i: (0, 0)),
        ],
        out_specs=pl.BlockSpec((steps, nb, cw), lambda i: (0, i, 0)),
        out_shape=jax.ShapeDtypeStruct((steps, n, cw), BF16),
        compiler_params=_cp("parallel"),
        name="conf_sample",
    )(x_tm, hist_tm, w_dw, vec(b_dw), vec(ln_w), vec(ln_b))


def _merge_kernel(oz_ref, cz_ref, ga_ref, gb_ref, wa_ref, wb_ref, bias_ref, o_ref):
    ya = jnp.dot(oz_ref[...], wa_ref[...], preferred_element_type=F32)
    yb = jnp.dot(cz_ref[...], wb_ref[...], preferred_element_type=F32) + bias_ref[...]
    o_ref[...] = (ga_ref[...].astype(F32) * ya + gb_ref[...].astype(F32) * yb).astype(o_ref.dtype)


def _merge(oz, cz, gates, wa, wb, bias):
    t, ka = oz.shape
    kb = cz.shape[1]
    d = wa.shape[1]
    nb = d // TN
    return pl.pallas_call(
        _merge_kernel,
        grid=(pl.cdiv(t, TM), nb),
        in_specs=[
            pl.BlockSpec((TM, ka), lambda i, j: (i, 0)),
            pl.BlockSpec((TM, kb), lambda i, j: (i, 0)),
            pl.BlockSpec((TM, TN), lambda i, j: (i, j)),
            pl.BlockSpec((TM, TN), lambda i, j: (i, j + nb)),
            pl.BlockSpec((ka, TN), lambda i, j: (0, j)),
            pl.BlockSpec((kb, TN), lambda i, j: (0, j)),
            pl.BlockSpec((1, TN), lambda i, j: (0, j)),
        ],
        out_specs=pl.BlockSpec((TM, TN), lambda i, j: (i, j)),
        out_shape=jax.ShapeDtypeStruct((t, d), BF16),
        compiler_params=_cp("parallel", "arbitrary"),
        name="merge",
    )(oz, cz, gates, gates, wa, wb, bias.reshape(1, d))


def _out_kernel(m_ref, w_ref, xp_ref, xt_ref, o_ref, *, prompt_blocks):
    acc = jnp.dot(m_ref[...], w_ref[...], preferred_element_type=F32)

    @pl.when(pl.program_id(0) < prompt_blocks)
    def _():
        o_ref[...] = xp_ref[...] + acc

    @pl.when(pl.program_id(0) >= prompt_blocks)
    def _():
        o_ref[...] = xt_ref[...] + acc


def _out_proj(merged, w_out, x_prompt, x_tail):
    t, k = merged.shape
    d = w_out.shape[1]
    nbp = x_prompt.shape[0] // TM
    return pl.pallas_call(
        functools.partial(_out_kernel, prompt_blocks=nbp),
        grid=(pl.cdiv(t, TM), d // TN),
        in_specs=[
            pl.BlockSpec((TM, k), lambda i, j: (i, 0)),
            pl.BlockSpec((k, TN), lambda i, j: (0, j)),
            pl.BlockSpec((TM, TN), lambda i, j: (jnp.minimum(i, nbp - 1), j)),
            pl.BlockSpec((TM, TN), lambda i, j: (jnp.maximum(i - nbp, 0), jnp.where(i >= nbp, j, 0))),
        ],
        out_specs=pl.BlockSpec((TM, TN), lambda i, j: (i, j)),
        out_shape=jax.ShapeDtypeStruct((t, d), F32),
        compiler_params=_cp("parallel", "arbitrary"),
        name="out_proj",
    )(merged, w_out, x_prompt, x_tail)


def _pack_bf16_pairs(x):
    half = x.shape[1] // 2
    bits = lax.bitcast_convert_type(x.astype(BF16).astype(F32), jnp.uint32)
    return (bits[:, :half] >> 16) | (bits[:, half:] & jnp.uint32(0xFFFF0000))


def _unpack_bf16_pairs(w):
    lo = lax.bitcast_convert_type(w << 16, F32).astype(BF16)
    hi = lax.bitcast_convert_type(w & jnp.uint32(0xFFFF0000), F32).astype(BF16)
    return lo, hi


def _router_kernel(h_ref, nw_ref, wr_ref, br_ref, xn_ref, ids_ref, wts_ref):
    x = h_ref[...]
    xn = x * lax.rsqrt(jnp.mean(x * x, axis=-1, keepdims=True) + EPS) * nw_ref[...]
    xn_ref[...] = _pack_bf16_pairs(xn)
    xh, xl = _split_bf16(xn)
    wh, wl = _split_bf16(wr_ref[...])
    dot = lambda a, b: jnp.dot(a, b, preferred_element_type=F32)
    logits = dot(xh, wh) + (dot(xh, wl) + dot(xl, wh)) + br_ref[...]
    lane = lax.broadcasted_iota(jnp.int32, logits.shape, 1)
    neg = -jnp.inf
    big = jnp.int32(1 << 20)
    gl = jnp.where(lane < N_GROUPS, logits, neg)
    gmax = jnp.max(gl, axis=-1, keepdims=True)
    gsum = jnp.sum(jnp.exp(gl - gmax), axis=-1, keepdims=True)
    pg_top = 1.0 / gsum
    gidx = jnp.min(jnp.where(gl == gmax, lane, big), axis=-1, keepdims=True)
    lo = N_GROUPS + gidx * EXPERTS_PER_GROUP
    in_grp = (lane >= lo) & (lane < lo + EXPERTS_PER_GROUP)
    el = jnp.where(in_grp, logits, neg)
    emax = jnp.max(el, axis=-1, keepdims=True)
    ex = jnp.exp(el - emax)
    esum = jnp.sum(ex, axis=-1, keepdims=True)
    pe = ex / esum
    pe = jnp.where(in_grp, pe, -1.0)
    p1 = jnp.max(pe, axis=-1, keepdims=True)
    i1 = jnp.min(jnp.where(pe == p1, lane, big), axis=-1, keepdims=True)
    pe2 = jnp.where(lane == i1, -1.0, pe)
    p2 = jnp.max(pe2, axis=-1, keepdims=True)
    i2 = jnp.min(jnp.where(pe2 == p2, lane, big), axis=-1, keepdims=True)
    denom = p1 + p2
    w1 = pg_top * p1 / denom
    w2 = pg_top * p2 / denom
    ids_ref[...] = jnp.where(lane == 0, i1 - N_GROUPS, jnp.where(lane == 1, i2 - N_GROUPS, 0))
    wts_ref[...] = jnp.where(lane == 0, w1, jnp.where(lane == 1, w2, 0.0))


def _router(h1, norm_w, w_r, b_r, rows=256):
    t, d = h1.shape
    return pl.pallas_call(
        _router_kernel,
        grid=(pl.cdiv(t, rows),),
        in_specs=[
            pl.BlockSpec((rows, d), lambda i: (i, 0)),
            pl.BlockSpec((1, d), lambda i: (0, 0)),
            pl.BlockSpec((d, LANES), lambda i: (0, 0)),
            pl.BlockSpec((1, LANES), lambda i: (0, 0)),
        ],
        out_specs=[
            pl.BlockSpec((rows, d // 2), lambda i: (i, 0)),
            pl.BlockSpec((rows, LANES), lambda i: (i, 0)),
            pl.BlockSpec((rows, LANES), lambda i: (i, 0)),
        ],
        out_shape=[
            jax.ShapeDtypeStruct((t, d // 2), jnp.uint32),
            jax.ShapeDtypeStruct((t, LANES), jnp.int32),
            jax.ShapeDtypeStruct((t, LANES), F32),
        ],
        compiler_params=_cp("parallel"),
        name="router",
    )(h1, norm_w.reshape(1, d), w_r, b_r)


def _experts_kernel(ve_ref, nv_ref, vr_ref, vs_ref, slot_ref, xn_hbm, wg_ref, wu_ref, wd_ref, eo_hbm,
                    xbuf, xlo, xhi, hmid, wdb, acc_ref, gsem, ssem):
    v = pl.program_id(0)
    f = pl.program_id(1)
    nf = pl.num_programs(1)
    nv = nv_ref[0]
    half = xbuf.shape[1]

    def gather_copy(visit, r):
        tok = slot_ref[vs_ref[visit] + r] >> 1
        return pltpu.make_async_copy(xn_hbm.at[pl.ds(tok, 1), :], xbuf.at[pl.ds(r, 1), :], gsem)

    def scatter_copy(visit, r):
        slot = slot_ref[vs_ref[visit] + r]
        return pltpu.make_async_copy(acc_ref.at[pl.ds(r, 1), :], eo_hbm.at[slot & 1, pl.ds(slot >> 1, 1), :], ssem)

    gather_rows_wait = lambda: pltpu.make_async_copy(
        xn_hbm.at[pl.ds(0, DMA_UNROLL), :], xbuf.at[pl.ds(0, DMA_UNROLL), :], gsem).wait()
    scatter_rows_wait = lambda: pltpu.make_async_copy(
        acc_ref.at[pl.ds(0, DMA_UNROLL), :], eo_hbm.at[0, pl.ds(0, DMA_UNROLL), :], ssem).wait()

    def loop(lo, hi, fn):
        def body(i, carry):
            fn(i)
            return carry
        lax.fori_loop(lo, hi, body, 0)

    def unrolled(fn):
        def run(i):
            for j in range(DMA_UNROLL):
                fn(i * DMA_UNROLL + j)
        return run

    def start_gather(visit):
        loop(0, (vr_ref[visit] + DMA_UNROLL - 1) // DMA_UNROLL, unrolled(lambda r: gather_copy(visit, r).start(priority=ROW_DMA_PRIORITY)))

    def wait_gather(visit):
        loop(0, (vr_ref[visit] + DMA_UNROLL - 1) // DMA_UNROLL, lambda i: gather_rows_wait())

    def start_scatter(visit):
        n = vr_ref[visit]
        loop(0, n // DMA_UNROLL, unrolled(lambda r: scatter_copy(visit, r).start(priority=ROW_DMA_PRIORITY)))
        loop((n // DMA_UNROLL) * DMA_UNROLL, n, lambda r: scatter_copy(visit, r).start(priority=ROW_DMA_PRIORITY))

    def wait_scatter(visit):
        n = vr_ref[visit]
        loop(0, n // DMA_UNROLL, lambda i: scatter_rows_wait())
        loop((n // DMA_UNROLL) * DMA_UNROLL, n, lambda r: scatter_copy(visit, r).wait())

    @pl.when((v == 0) & (f == 0))
    def _():
        xbuf[...] = jnp.zeros_like(xbuf)
        start_gather(0)

    @pl.when((v < nv) & (f == 0))
    def _():
        wait_gather(v)
        lo, hi = _unpack_bf16_pairs(xbuf[...])
        xlo[...] = lo
        xhi[...] = hi

        @pl.when(v + 1 < nv)
        def _():
            start_gather(v + 1)

    @pl.when(v < nv)
    def _():
        wg = wg_ref[0].astype(BF16)
        wu = wu_ref[0].astype(BF16)
        xa = xlo[...]
        xb = xhi[...]
        g = jnp.dot(xa, wg[:half], preferred_element_type=F32) + jnp.dot(xb, wg[half:], preferred_element_type=F32)
        u = jnp.dot(xa, wu[:half], preferred_element_type=F32) + jnp.dot(xb, wu[half:], preferred_element_type=F32)
        hmid[f] = (_silu(g) * u).astype(BF16)
        wdb[f] = wd_ref[0].astype(BF16)

        @pl.when(f == nf - 1)
        def _():
            @pl.when(v > 0)
            def _():
                wait_scatter(v - 1)
            out = jnp.dot(hmid[0], wdb[0], preferred_element_type=F32)
            for j in range(1, hmid.shape[0]):
                out = out + jnp.dot(hmid[j], wdb[j], preferred_element_type=F32)
            acc_ref[...] = _pack_bf16_pairs(out)
            start_scatter(v)

            @pl.when(v == nv - 1)
            def _():
                wait_scatter(v)


def _experts(visit_expert, n_visits, visit_rows, visit_start, row_slot, xn_packed, w_gate, w_up, w_down, max_visits):
    t, half = xn_packed.shape
    d = 2 * half
    fdim = w_gate.shape[2]
    nf = fdim // EXPERT_F_TILE
    ftile = lambda v, f, nv: jnp.where(v < nv[0], f, nf - 1)
    grid_spec = pltpu.PrefetchScalarGridSpec(
        num_scalar_prefetch=5,
        grid=(max_visits, nf),
        in_specs=[
            pl.BlockSpec(memory_space=pl.ANY),
            pl.BlockSpec((1, d, EXPERT_F_TILE), lambda v, f, ve, nv, vr, vs, sl: (ve[v], 0, ftile(v, f, nv))),
            pl.BlockSpec((1, d, EXPERT_F_TILE), lambda v, f, ve, nv, vr, vs, sl: (ve[v], 0, ftile(v, f, nv))),
            pl.BlockSpec((1, EXPERT_F_TILE, d), lambda v, f, ve, nv, vr, vs, sl: (ve[v], ftile(v, f, nv), 0)),
        ],
        out_specs=pl.BlockSpec(memory_space=pl.ANY),
        scratch_shapes=[
            pltpu.VMEM((GROUP_ROWS, half), jnp.uint32),
            pltpu.VMEM((GROUP_ROWS, half), BF16),
            pltpu.VMEM((GROUP_ROWS, half), BF16),
            pltpu.VMEM((nf, GROUP_ROWS, EXPERT_F_TILE), BF16),
            pltpu.VMEM((nf, EXPERT_F_TILE, d), BF16),
            pltpu.VMEM((GROUP_ROWS, half), jnp.uint32),
            pltpu.SemaphoreType.DMA(()),
            pltpu.SemaphoreType.DMA(()),
        ],
    )
    return pl.pallas_call(
        _experts_kernel,
        grid_spec=grid_spec,
        out_shape=jax.ShapeDtypeStruct((2, t, half), jnp.uint32),
        compiler_params=_cp("arbitrary", "arbitrary"),
        name="experts",
    )(visit_expert, n_visits, visit_rows, visit_start, row_slot, xn_packed, w_gate, w_up, w_down)


def _final_kernel(h_ref, o1_ref, o2_ref, wts_ref, nw_ref, y_ref):
    wts = wts_ref[...]
    unpack = lambda w: jnp.concatenate([p.astype(F32) for p in _unpack_bf16_pairs(w)], axis=-1)
    h = h_ref[...] + wts[:, 0:1] * unpack(o1_ref[0]) + wts[:, 1:2] * unpack(o2_ref[0])
    y_ref[...] = h * lax.rsqrt(jnp.mean(h * h, axis=-1, keepdims=True) + EPS) * nw_ref[...]


def _final(h1, eo, wts, norm_w, n_rows, row_block0, rows=FINAL_ROWS):
    d = h1.shape[1]
    rb = lambda i: (i + row_block0, 0)
    return pl.pallas_call(
        _final_kernel,
        grid=(n_rows // rows,),
        in_specs=[
            pl.BlockSpec((rows, d), rb),
            pl.BlockSpec((1, rows, d // 2), lambda i: (0, i + row_block0, 0)),
            pl.BlockSpec((1, rows, d // 2), lambda i: (1, i + row_block0, 0)),
            pl.BlockSpec((rows, LANES), rb),
            pl.BlockSpec((1, d), lambda i: (0, 0)),
        ],
        out_specs=pl.BlockSpec((rows, d), lambda i: (i, 0)),
        out_shape=jax.ShapeDtypeStruct((n_rows, d), F32),
        compiler_params=_cp("parallel"),
        name="final",
    )(h1, eo, eo, wts, norm_w.reshape(1, d))


def _dispatch_plan(ids, max_visits):
    flat_e = ids[:, :2].reshape(-1)
    order = jnp.argsort(flat_e, stable=True).astype(jnp.int32)
    counts = jnp.sum((flat_e[:, None] == jnp.arange(N_EXPERTS, dtype=jnp.int32)[None, :]).astype(jnp.int32), axis=0)
    tiles = (counts + GROUP_ROWS - 1) // GROUP_ROWS
    tile_end = jnp.cumsum(tiles)
    tile_start = tile_end - tiles
    group_start = jnp.cumsum(counts) - counts
    n_visits = tile_end[-1]
    visit = jnp.arange(max_visits, dtype=jnp.int32)
    visit_expert = jnp.sum((tile_end[None, :] <= jnp.minimum(visit, n_visits - 1)[:, None]).astype(jnp.int32), axis=1)
    visit_expert = jnp.minimum(visit_expert, N_EXPERTS - 1)
    tile_in_expert = visit - tile_start[visit_expert]
    rows_left = counts[visit_expert] - GROUP_ROWS * tile_in_expert
    valid = visit < n_visits
    visit_rows = jnp.where(valid, jnp.clip(rows_left, 0, GROUP_ROWS), 0).astype(jnp.int32)
    visit_start = jnp.where(valid, group_start[visit_expert] + GROUP_ROWS * tile_in_expert, 0).astype(jnp.int32)
    row_slot = jnp.pad(order, (0, DMA_UNROLL))
    return row_slot, visit_expert.astype(jnp.int32), n_visits.reshape(1).astype(jnp.int32), visit_rows, visit_start


def kernel(x_prompt, x_sample, state_delta, state_conv_qkv, state_conv_b, meta_tokens,
           norm1, w_in, w_conv_qkv, a_log, dt_bias, w_onorm, w_proj_a, w_dw, b_dw, ln_w, ln_b,
           w_proj_b, b_proj_b, w_out, norm2, w_rg, b_rg, w_re, b_re, w_e_gate, w_e_up, w_e_down,
           final_norm):
    bsz, seq, d = x_prompt.shape
    nsmp, steps, _ = x_sample.shape
    n_meta = meta_tokens.shape[0]
    depth = norm1.shape[0]
    assert depth == 1
    tp = bsz * seq
    ts = nsmp * steps
    t_all = tp + ts + n_meta
    qkv_w = 3 * DN_WIDTH
    conf_w = w_dw.shape[-1]
    hist = CONF_KERNEL - 1

    x_p = x_prompt.reshape(tp, d)
    tail_rows = pl.cdiv(ts + n_meta, TM) * TM
    x_tail = jnp.concatenate([x_sample.transpose(1, 0, 2).reshape(ts, d), meta_tokens,
                              jnp.zeros((tail_rows - ts - n_meta, d), F32)], axis=0)

    wi = w_in[0]
    o_z = qkv_w + DN_WIDTH
    o_glu = o_z + 2 * N_HEADS
    o_gate = o_glu + 2 * conf_w
    w_qkvz = wi[:, :o_z].astype(BF16)
    w_ab = jnp.pad(wi[:, o_z:o_glu], ((0, 0), (0, LANES - 2 * N_HEADS))).astype(BF16)
    w_glu = wi[:, o_glu:o_gate].astype(BF16)
    w_gate = wi[:, o_gate:].astype(BF16)
    alog_pad = jnp.pad(a_log[0], (0, LANES - N_HEADS)).reshape(1, LANES)
    dtb_pad = jnp.pad(dt_bias[0], (0, LANES - N_HEADS)).reshape(1, LANES)

    hn = _rmsnorm_bf16(x_p, x_tail, norm1[0], t_all)
    qkvz = _proj(_mm_plain_kernel, hn, w_qkvz, o_z, F32, "proj_qkvz")
    gbeta = _proj_decay(hn, w_ab, alog_pad, dtb_pad)
    glu = _proj_glu(hn, w_glu)
    gates = _proj(_mm_sigmoid_kernel, hn, w_gate, 2 * d, BF16, "proj_gates")

    wc = w_conv_qkv[0]
    meta_blk = (tp + ts) // n_meta
    qkv_meta = _prep_long(qkvz, wc, n_meta, n_meta, meta_blk, lambda i: 0, True, "prep_meta")
    rows_p = 256
    bps = seq // rows_p
    meta_halo = (tp + ts + n_meta) // 8 - 1
    qkv_p = _prep_long(qkvz, wc, tp, rows_p, 0,
                       lambda i: jnp.where(i % bps == 0, meta_halo, i * (rows_p // 8) - 1), False, "prep_prompt")
    raw_s = qkvz[tp:tp + ts, :qkv_w].reshape(steps, nsmp, qkv_w)
    st_qkv_tm = state_conv_qkv[0].transpose(1, 0, 2)
    qkv_s_tm = _prep_short(raw_s, st_qkv_tm, wc)

    zero_state = jnp.zeros((1, N_HEADS, HEAD_DIM, HEAD_DIM), F32)
    oz_meta, s_meta = _delta(qkv_meta, gbeta, qkvz, 3, zero_state, w_onorm[0], 1, 1, n_meta,
                             lambda n, ci: 0, lambda n, ci: meta_blk, lambda n: 0, "delta_meta")
    n_chunks = seq // CHUNK
    prompt_blk = lambda n, ci: n * n_chunks + ci
    oz_p, s_p = _delta(qkv_p, gbeta, qkvz, 3, s_meta, w_onorm[0], bsz, n_chunks, CHUNK,
                       prompt_blk, prompt_blk, lambda n: 0, "delta_prompt")

    cpad = 8
    to_bm = lambda a: jnp.pad(a.transpose(1, 0, 2), ((0, 0), (0, cpad - steps), (0, 0))).reshape(nsmp * cpad, a.shape[-1])
    qkv_s = to_bm(qkv_s_tm)
    gb_s = to_bm(gbeta[tp:tp + ts].reshape(steps, nsmp, LANES))
    z_s = to_bm(qkvz[tp:tp + ts, qkv_w:].reshape(steps, nsmp, DN_WIDTH))
    oz_s_bm, s_s = _delta(qkv_s, gb_s, z_s, 0, state_delta[0], w_onorm[0], nsmp, 1, cpad,
                          lambda n, ci: n, lambda n, ci: n, lambda n: n, "delta_sample")
    oz_s = oz_s_bm.reshape(nsmp, cpad, DN_WIDTH)[:, :steps].transpose(1, 0, 2).reshape(ts, DN_WIDTH)
    oz = jnp.concatenate([oz_p, oz_s, oz_meta], axis=0)

    zeros32 = jnp.zeros((32, conf_w), F32)
    cz_meta = _conf_long(glu, zeros32, w_dw[0], b_dw[0], ln_w[0], ln_b[0], n_meta, n_meta, meta_blk, 1,
                         lambda i: 0, "conf_meta")
    init_p = jnp.concatenate([jnp.zeros((32 - n_meta, conf_w), F32), glu[tp + ts:]], axis=0)
    rows_c = 128
    cz_p = _conf_long(glu, init_p, w_dw[0], b_dw[0], ln_w[0], ln_b[0], tp, rows_c, 0, seq // rows_c,
                      lambda i: jnp.maximum(i * (rows_c // 32) - 1, 0), "conf_prompt")
    glu_s_tm = glu[tp:tp + ts].reshape(steps, nsmp, conf_w)
    hist_s_tm = state_conv_b[0].transpose(1, 0, 2)
    cz_s = _conf_short(glu_s_tm, hist_s_tm, w_dw[0], b_dw[0], ln_w[0], ln_b[0]).reshape(ts, conf_w)
    cz = jnp.concatenate([cz_p, cz_s, cz_meta], axis=0)

    merged = _merge(oz, cz, gates, w_proj_a[0].astype(BF16), w_proj_b[0].astype(BF16), b_proj_b[0])
    h1 = _out_proj(merged, w_out[0].astype(BF16), x_p, x_tail)

    w_r = jnp.pad(jnp.concatenate([w_rg[0], w_re[0]], axis=1), ((0, 0), (0, LANES - N_GROUPS - N_EXPERTS)))
    b_r = jnp.pad(jnp.concatenate([b_rg[0], b_re[0]]), (0, LANES - N_GROUPS - N_EXPERTS)).reshape(1, LANES)
    xn2, ids, wts = _router(h1, norm2[0], w_r, b_r)
    max_visits = N_EXPERTS + (2 * t_all) // GROUP_ROWS
    row_slot, visit_expert, n_visits, visit_rows, visit_start = _dispatch_plan(ids, max_visits)
    eo = _experts(visit_expert, n_visits, visit_rows, visit_start, row_slot, xn2,
                  w_e_gate[0], w_e_up[0], w_e_down[0], max_visits)
    y_p = _final(h1, eo, wts, final_norm, tp, 0)
    y_s = _final(h1, eo, wts, final_norm, ts, tp // FINAL_ROWS)

    y_prompt = y_p.reshape(bsz, seq, d)
    y_sample = y_s.reshape(steps, nsmp, d).transpose(1, 0, 2)
    new_cq_p = jnp.stack([qkvz[(b + 1) * seq - (SHORT_CONV - 1):(b + 1) * seq, :qkv_w] for b in range(bsz)])
    new_cb_p = jnp.stack([glu[(b + 1) * seq - hist:(b + 1) * seq] for b in range(bsz)])
    cq_s = jnp.concatenate([st_qkv_tm, raw_s], axis=0)[-(SHORT_CONV - 1):].transpose(1, 0, 2)
    cb_s = jnp.concatenate([hist_s_tm, glu_s_tm], axis=0)[-hist:].transpose(1, 0, 2)
    return (y_prompt, y_sample, s_p[None], new_cq_p[None], new_cb_p[None], s_s[None], cq_s[None], cb_s[None])
```

```python
import functools

import jax
import jax.numpy as jnp
from jax import lax
from jax.experimental import pallas as pl
from jax.experimental.pallas import tpu as pltpu

F32 = jnp.float32
BF16 = jnp.bfloat16
EPS = 1e-6

N_HEADS = 16
HEAD_DIM = 128
DN_WIDTH = N_HEADS * HEAD_DIM
SHORT_CONV = 4
CONF_KERNEL = 31
N_GROUPS = 8
EXPERTS_PER_GROUP = 8
N_EXPERTS = N_GROUPS * EXPERTS_PER_GROUP
CHUNK = 64
LANES = 128
HIGHEST = lax.Precision.HIGHEST

TM = 1024
TN = 512
GROUP_ROWS = 320
EXPERT_F_TILE = 256
DMA_UNROLL = 8
ROW_DMA_PRIORITY = 1
CONF_LANE_CHUNK = 512
FINAL_ROWS = 256
VMEM_LIMIT = 56 * 1024 * 1024


def _cp(*sem):
    return pltpu.CompilerParams(dimension_semantics=sem, vmem_limit_bytes=VMEM_LIMIT)


def _sigmoid(x):
    return 1.0 / (1.0 + jnp.exp(-x))


def _silu(x):
    return x * _sigmoid(x)


def _softplus(x):
    return jnp.maximum(x, 0.0) + jnp.log1p(jnp.exp(-jnp.abs(x)))


def _rmsnorm_kernel(xp_ref, xt_ref, w_ref, o_ref, *, prompt_blocks):
    def norm(x):
        ms = jnp.mean(x * x, axis=-1, keepdims=True)
        return (x * lax.rsqrt(ms + EPS) * w_ref[...]).astype(o_ref.dtype)

    @pl.when(pl.program_id(0) < prompt_blocks)
    def _():
        o_ref[...] = norm(xp_ref[...])

    @pl.when(pl.program_id(0) >= prompt_blocks)
    def _():
        o_ref[...] = norm(xt_ref[...])


def _rmsnorm_bf16(x_prompt, x_tail, w, t_all, rows=256):
    tp, d = x_prompt.shape
    nbp = tp // rows
    return pl.pallas_call(
        functools.partial(_rmsnorm_kernel, prompt_blocks=nbp),
        grid=(pl.cdiv(t_all, rows),),
        in_specs=[
            pl.BlockSpec((rows, d), lambda i: (jnp.minimum(i, nbp - 1), 0)),
            pl.BlockSpec((rows, d), lambda i: (jnp.maximum(i - nbp, 0), 0)),
            pl.BlockSpec((1, d), lambda i: (0, 0)),
        ],
        out_specs=pl.BlockSpec((rows, d), lambda i: (i, 0)),
        out_shape=jax.ShapeDtypeStruct((t_all, d), BF16),
        compiler_params=_cp("parallel"),
        name="rmsnorm1",
    )(x_prompt, x_tail, w.reshape(1, d))


def _mm_plain_kernel(x_ref, w_ref, o_ref):
    o_ref[...] = jnp.dot(x_ref[...], w_ref[...], preferred_element_type=F32).astype(o_ref.dtype)


def _mm_sigmoid_kernel(x_ref, w_ref, o_ref):
    o_ref[...] = _sigmoid(jnp.dot(x_ref[...], w_ref[...], preferred_element_type=F32)).astype(o_ref.dtype)


def _mm_glu_kernel(x_ref, wu_ref, wg_ref, o_ref):
    x = x_ref[...]
    u = jnp.dot(x, wu_ref[...], preferred_element_type=F32)
    g = jnp.dot(x, wg_ref[...], preferred_element_type=F32)
    o_ref[...] = u * _sigmoid(g)


def _mm_decay_kernel(x_ref, w_ref, alog_ref, dtb_ref, o_ref):
    acc = jnp.dot(x_ref[...], w_ref[...], preferred_element_type=F32)
    lane = lax.broadcasted_iota(jnp.int32, acc.shape, 1)
    g = -jnp.exp(alog_ref[...]) * _softplus(acc + dtb_ref[...])
    beta = _sigmoid(acc)
    o_ref[...] = jnp.where(lane < N_HEADS, g, jnp.where(lane < 2 * N_HEADS, beta, 0.0))


def _proj(kernel_fn, hn, w, n_out, out_dtype, name, tn=TN):
    t, k = hn.shape
    return pl.pallas_call(
        kernel_fn,
        grid=(pl.cdiv(t, TM), n_out // tn),
        in_specs=[pl.BlockSpec((TM, k), lambda i, j: (i, 0)), pl.BlockSpec((k, tn), lambda i, j: (0, j))],
        out_specs=pl.BlockSpec((TM, tn), lambda i, j: (i, j)),
        out_shape=jax.ShapeDtypeStruct((t, n_out), out_dtype),
        compiler_params=_cp("parallel", "arbitrary"),
        name=name,
    )(hn, w)


def _proj_glu(hn, w_glu):
    t, k = hn.shape
    half = w_glu.shape[1] // 2
    nb = half // TN
    return pl.pallas_call(
        _mm_glu_kernel,
        grid=(pl.cdiv(t, TM), nb),
        in_specs=[
            pl.BlockSpec((TM, k), lambda i, j: (i, 0)),
            pl.BlockSpec((k, TN), lambda i, j: (0, j)),
            pl.BlockSpec((k, TN), lambda i, j: (0, j + nb)),
        ],
        out_specs=pl.BlockSpec((TM, TN), lambda i, j: (i, j)),
        out_shape=jax.ShapeDtypeStruct((t, half), F32),
        compiler_params=_cp("parallel", "arbitrary"),
        name="proj_glu",
    )(hn, w_glu, w_glu)


def _proj_decay(hn, w_ab, alog_pad, dtb_pad):
    t, k = hn.shape
    return pl.pallas_call(
        _mm_decay_kernel,
        grid=(pl.cdiv(t, TM),),
        in_specs=[
            pl.BlockSpec((TM, k), lambda i: (i, 0)),
            pl.BlockSpec((k, LANES), lambda i: (0, 0)),
            pl.BlockSpec((1, LANES), lambda i: (0, 0)),
            pl.BlockSpec((1, LANES), lambda i: (0, 0)),
        ],
        out_specs=pl.BlockSpec((TM, LANES), lambda i: (i, 0)),
        out_shape=jax.ShapeDtypeStruct((t, LANES), F32),
        compiler_params=_cp("parallel"),
        name="proj_decay",
    )(hn, w_ab, alog_pad, dtb_pad)


def _head_normalize(y, sec):
    outs = []
    for h in range(N_HEADS):
        yh = y[:, h * HEAD_DIM:(h + 1) * HEAD_DIM]
        inv = lax.rsqrt(jnp.sum(yh * yh, axis=-1, keepdims=True) + EPS)
        scale = jnp.where(sec == 0, inv * (HEAD_DIM ** -0.5), jnp.where(sec == 1, inv, 1.0))
        outs.append(yh * scale)
    return jnp.concatenate(outs, axis=-1)


def _prep_long_kernel(cur_ref, halo_ref, w_ref, o_ref, ext_ref, *, rows, zero_halo):
    sec = pl.program_id(1)
    halo = halo_ref[...]
    ext_ref[0:8, :] = jnp.zeros_like(halo) if zero_halo else halo
    ext_ref[8:, :] = cur_ref[...]
    acc = w_ref[SHORT_CONV - 1:SHORT_CONV, :] * cur_ref[...]
    for s in range(1, SHORT_CONV):
        acc = acc + w_ref[SHORT_CONV - 1 - s:SHORT_CONV - s, :] * ext_ref[pl.ds(8 - s, rows), :]
    o_ref[...] = _head_normalize(_silu(acc), sec)


def _prep_long(qkvz, w_conv, n_rows, rows, row_block0, halo_index, zero_halo, name):
    nblk = n_rows // rows
    kern = functools.partial(_prep_long_kernel, rows=rows, zero_halo=zero_halo)
    return pl.pallas_call(
        kern,
        grid=(nblk, 3),
        in_specs=[
            pl.BlockSpec((rows, DN_WIDTH), lambda i, s: (i + row_block0, s)),
            pl.BlockSpec((8, DN_WIDTH), lambda i, s: (halo_index(i), s)),
            pl.BlockSpec((SHORT_CONV, DN_WIDTH), lambda i, s: (0, s)),
        ],
        out_specs=pl.BlockSpec((rows, DN_WIDTH), lambda i, s: (i, s)),
        out_shape=jax.ShapeDtypeStruct((n_rows, 3 * DN_WIDTH), F32),
        scratch_shapes=[pltpu.VMEM((rows + 8, DN_WIDTH), F32)],
        compiler_params=_cp("parallel", "arbitrary"),
        name=name,
    )(qkvz, qkvz, w_conv)


def _prep_short_kernel(x_ref, st_ref, w_ref, o_ref, *, steps):
    sec = pl.program_id(0)
    buf = [st_ref[i] for i in range(SHORT_CONV - 1)] + [x_ref[t] for t in range(steps)]
    for t in range(steps):
        acc = w_ref[0:1, :] * buf[t]
        for i in range(1, SHORT_CONV):
            acc = acc + w_ref[i:i + 1, :] * buf[t + i]
        o_ref[t] = _head_normalize(_silu(acc), sec)


def _prep_short(x_tm, state_tm, w_conv):
    steps, n, _ = x_tm.shape
    kern = functools.partial(_prep_short_kernel, steps=steps)
    return pl.pallas_call(
        kern,
        grid=(3,),
        in_specs=[
            pl.BlockSpec((steps, n, DN_WIDTH), lambda s: (0, 0, s)),
            pl.BlockSpec((SHORT_CONV - 1, n, DN_WIDTH), lambda s: (0, 0, s)),
            pl.BlockSpec((SHORT_CONV, DN_WIDTH), lambda s: (0, s)),
        ],
        out_specs=pl.BlockSpec((steps, n, DN_WIDTH), lambda s: (0, 0, s)),
        out_shape=jax.ShapeDtypeStruct((steps, n, 3 * DN_WIDTH), F32),
        compiler_params=_cp("arbitrary"),
        name="prep_sample",
    )(x_tm, state_tm, w_conv)


def _hdot(a, b):
    return jnp.dot(a, b, preferred_element_type=F32, precision=HIGHEST)


def _split_bf16(a):
    hi = a.astype(BF16)
    lo = (a - hi.astype(F32)).astype(BF16)
    return hi, lo


def _bmm(a, b):
    return jnp.einsum("hik,hkj->hij", a.astype(BF16), b.astype(BF16), preferred_element_type=F32)


def _bmm3(a, b):
    ah, al = _split_bf16(a)
    bh, bl = _split_bf16(b)
    f = lambda x, y: jnp.einsum("hik,hkj->hij", x, y, preferred_element_type=F32)
    return f(ah, bh) + (f(ah, bl) + f(al, bh))


def _unit_lower_inverse(low, c):
    base = min(c, 16)
    row = lax.broadcasted_iota(jnp.int32, (c, c), 0)
    col = lax.broadcasted_iota(jnp.int32, (c, c), 1)
    eye = (row == col).astype(F32)[None]
    diag = jnp.where(((row // base) == (col // base))[None], low, 0.0) if c > base else low
    inv = eye - diag
    power = diag
    k = 2
    while k < base:
        power = _bmm3(power, power)
        inv = inv + _bmm3(inv, power)
        k *= 2
    blk = base
    while blk < c:
        sel = ((row // (2 * blk)) == (col // (2 * blk))) & (((row // blk) % 2) == 1) & (((col // blk) % 2) == 0)
        off = jnp.where(sel[None], low, 0.0)
        inv = inv - _bmm3(inv, _bmm3(off, inv))
        blk *= 2
    return inv


def _delta_kernel(q_ref, k_ref, v_ref, gb_ref, z_ref, s0_ref, wn_ref, o_ref, sout_ref, s_scr, *, c, heads_per_group):
    ci = pl.program_id(1)

    @pl.when(ci == 0)
    def _():
        s_scr[...] = s0_ref[0]

    row = lax.broadcasted_iota(jnp.int32, (c, c), 0)
    col = lax.broadcasted_iota(jnp.int32, (c, c), 1)
    causal = (row >= col)[None]
    strict = (row > col)[None]
    gb = gb_ref[...]
    gcum = _hdot((row >= col).astype(F32), gb)
    gcum_t = gcum.T
    wn = wn_ref[...]
    d = HEAD_DIM
    for h0 in range(0, N_HEADS, heads_per_group):
        heads = range(h0, h0 + heads_per_group)
        hs = slice(h0, h0 + heads_per_group)
        heads_of = lambda ref: jnp.stack([ref[:, h * d:(h + 1) * d] for h in heads])
        q = heads_of(q_ref)
        k = heads_of(k_ref)
        v = heads_of(v_ref)
        g_col = jnp.stack([gcum[:, h:h + 1] for h in heads])
        g_row = jnp.stack([gcum_t[h:h + 1, :] for h in heads])
        beta = jnp.stack([gb[:, N_HEADS + h:N_HEADS + h + 1] for h in heads])
        g_last = g_col[:, c - 1:c, :]
        decay = jnp.where(causal, jnp.exp(g_col - g_row), 0.0)
        exp_g = jnp.exp(g_col)
        kb = k * beta
        kk = jnp.einsum("hid,hjd->hij", jnp.concatenate([kb, q], axis=1).astype(BF16), k.astype(BF16),
                        preferred_element_type=F32)
        lower = jnp.where(strict, kk[:, :c] * decay, 0.0)
        attn = kk[:, c:] * decay
        tinv = _unit_lower_inverse(lower, c)
        rhs = jnp.concatenate([v * beta, kb * exp_g], axis=-1)
        sol = _bmm3(tinv, rhs)
        s = s_scr[hs]
        both = _bmm(jnp.concatenate([sol[:, :, d:], q * exp_g], axis=1), s)
        u = sol[:, :, :d] - both[:, :c]
        o = both[:, c:] + _bmm(attn, u)
        kd = k * jnp.exp(g_last - g_col)
        upd = lax.dot_general(kd.astype(BF16), u.astype(BF16), (((1,), (1,)), ((0,), (0,))),
                              preferred_element_type=F32)
        s_scr[hs] = s * jnp.exp(g_last) + upd
        on = o * lax.rsqrt(jnp.mean(o * o, axis=-1, keepdims=True) + EPS) * wn
        out = (on * _silu(heads_of(z_ref))).astype(o_ref.dtype)
        for i, h in enumerate(heads):
            o_ref[:, h * d:(h + 1) * d] = out[i]

    @pl.when(ci == pl.num_programs(1) - 1)
    def _():
        sout_ref[0] = s_scr[...]


def _delta(qkv, gb, zsrc, z_col_block, s0, w_onorm, n_seq, n_chunks, c, row_block, aux_block, s0_index, name,
           heads_per_group=N_HEADS):
    kern = functools.partial(_delta_kernel, c=c, heads_per_group=heads_per_group)
    rows_out = n_seq * n_chunks * c
    return pl.pallas_call(
        kern,
        grid=(n_seq, n_chunks),
        in_specs=[
            pl.BlockSpec((c, DN_WIDTH), lambda n, ci: (row_block(n, ci), 0)),
            pl.BlockSpec((c, DN_WIDTH), lambda n, ci: (row_block(n, ci), 1)),
            pl.BlockSpec((c, DN_WIDTH), lambda n, ci: (row_block(n, ci), 2)),
            pl.BlockSpec((c, LANES), lambda n, ci: (aux_block(n, ci), 0)),
            pl.BlockSpec((c, DN_WIDTH), lambda n, ci: (aux_block(n, ci), z_col_block)),
            pl.BlockSpec((1, N_HEADS, HEAD_DIM, HEAD_DIM), lambda n, ci: (s0_index(n), 0, 0, 0)),
            pl.BlockSpec((1, HEAD_DIM), lambda n, ci: (0, 0)),
        ],
        out_specs=[
            pl.BlockSpec((c, DN_WIDTH), lambda n, ci: (n * n_chunks + ci, 0)),
            pl.BlockSpec((1, N_HEADS, HEAD_DIM, HEAD_DIM), lambda n, ci: (n, 0, 0, 0)),
        ],
        out_shape=[
            jax.ShapeDtypeStruct((rows_out, DN_WIDTH), BF16),
            jax.ShapeDtypeStruct((n_seq, N_HEADS, HEAD_DIM, HEAD_DIM), F32),
        ],
        scratch_shapes=[pltpu.VMEM((N_HEADS, HEAD_DIM, HEAD_DIM), F32)],
        compiler_params=_cp("parallel", "arbitrary"),
        name=name,
    )(qkv, qkv, qkv, gb, zsrc, s0, w_onorm.reshape(1, HEAD_DIM))


def _layernorm_silu(c, lnw, lnb):
    mu = jnp.mean(c, axis=-1, keepdims=True)
    xc = c - mu
    y = xc * lax.rsqrt(jnp.mean(xc * xc, axis=-1, keepdims=True) + EPS)
    return _silu(y * lnw + lnb)


def _conf_long_kernel(cur_ref, prev_ref, init_ref, w_ref, b_ref, lnw_ref, lnb_ref, o_ref, ext_ref, c_ref, *, rows, blocks_per_seq):
    hist = CONF_KERNEL - 1
    first = (pl.program_id(0) % blocks_per_seq) == 0
    ext_ref[0:32, :] = jnp.where(first, init_ref[...], prev_ref[...])
    ext_ref[32:, :] = cur_ref[...]
    cw = cur_ref.shape[1]
    lc = min(CONF_LANE_CHUNK, cw)
    rowid = lax.broadcasted_iota(jnp.int32, (8, lc), 0)
    for c0 in range(0, cw, lc):
        cs = slice(c0, c0 + lc)

        def qtile(n, r):
            acc = None
            for a in range(5):
                i = 8 * a + r - 2
                if 0 <= i <= hist:
                    term = w_ref[8 * i:8 * i + 8, cs] * ext_ref[8 * (n + a):8 * (n + a) + 8, cs]
                    acc = term if acc is None else acc + term
            return acc

        held = [None] + [qtile(0, r) for r in range(1, 8)]
        for m in range(rows // 8):
            out = qtile(m, 0)
            for r in range(1, 8):
                nxt = qtile(m + 1, r)
                out = out + pltpu.roll(jnp.where(rowid >= r, held[r], nxt), 8 - r, axis=0)
                held[r] = nxt
            c_ref[8 * m:8 * m + 8, cs] = out
    o_ref[...] = _layernorm_silu(c_ref[...] + b_ref[...], lnw_ref[...], lnb_ref[...]).astype(o_ref.dtype)


def _conf_long(glu, init_hist, w_dw, b_dw, ln_w, ln_b, n_rows, rows, row_block0, blocks_per_seq, prev_index, name):
    cw = glu.shape[1]
    nblk = n_rows // rows
    kern = functools.partial(_conf_long_kernel, rows=rows, blocks_per_seq=blocks_per_seq)
    vec = lambda a: a.reshape(1, cw)
    return pl.pallas_call(
        kern,
        grid=(nblk,),
        in_specs=[
            pl.BlockSpec((rows, cw), lambda i: (i + row_block0, 0)),
            pl.BlockSpec((32, cw), lambda i: (prev_index(i), 0)),
            pl.BlockSpec((32, cw), lambda i: (0, 0)),
            pl.BlockSpec((8 * CONF_KERNEL, cw), lambda i: (0, 0)),
            pl.BlockSpec((1, cw), lambda i: (0, 0)),
            pl.BlockSpec((1, cw), lambda i: (0, 0)),
            pl.BlockSpec((1, cw), lambda i: (0, 0)),
        ],
        out_specs=pl.BlockSpec((rows, cw), lambda i: (i, 0)),
        out_shape=jax.ShapeDtypeStruct((n_rows, cw), BF16),
        scratch_shapes=[pltpu.VMEM((rows + 32, cw), F32), pltpu.VMEM((rows, cw), F32)],
        compiler_params=_cp("parallel"),
        name=name,
    )(glu, glu, init_hist, jnp.repeat(w_dw, 8, axis=0), vec(b_dw), vec(ln_w), vec(ln_b))


def _conf_short_kernel(x_ref, hist_ref, w_ref, b_ref, lnw_ref, lnb_ref, o_ref, *, steps):
    nh = CONF_KERNEL - 1
    for t in range(steps):
        acc = jnp.zeros(x_ref.shape[1:], F32)
        for i in range(CONF_KERNEL):
            j = t + i
            src = hist_ref[j] if j < nh else x_ref[j - nh]
            acc = acc + w_ref[i:i + 1, :] * src
        o_ref[t] = _layernorm_silu(acc + b_ref[...], lnw_ref[...], lnb_ref[...]).astype(o_ref.dtype)


def _conf_short(x_tm, hist_tm, w_dw, b_dw, ln_w, ln_b, nb=32):
    steps, n, cw = x_tm.shape
    kern = functools.partial(_conf_short_kernel, steps=steps)
    vec = lambda a: a.reshape(1, cw)
    return pl.pallas_call(
        kern,
        grid=(n // nb,),
        in_specs=[
            pl.BlockSpec((steps, nb, cw), lambda i: (0, i, 0)),
            pl.BlockSpec((CONF_KERNEL - 1, nb, cw), lambda i: (0, i, 0)),
            pl.BlockSpec((CONF_KERNEL, cw), lambda i: (0, 0)),
            pl.BlockSpec((1, cw), lambda i: (0, 0)),
            pl.BlockSpec((1, cw), lambda i: (0, 0)),
            pl.BlockSpec((1, cw), lambda i: (0, 0)),
        ],
        out_specs=pl.BlockSpec((steps, nb, cw), lambda i: (0, i, 0)),
        out_shape=jax.ShapeDtypeStruct((steps, n, cw), BF16),
        compiler_params=_cp("parallel"),
        name="conf_sample",
    )(x_tm, hist_tm, w_dw, vec(b_dw), vec(ln_w), vec(ln_b))


def _merge_kernel(oz_ref, cz_ref, ga_ref, gb_ref, wa_ref, wb_ref, bias_ref, o_ref):
    ya = jnp.dot(oz_ref[...], wa_ref[...], preferred_element_type=F32)
    yb = jnp.dot(cz_ref[...], wb_ref[...], preferred_element_type=F32) + bias_ref[...]
    o_ref[...] = (ga_ref[...].astype(F32) * ya + gb_ref[...].astype(F32) * yb).astype(o_ref.dtype)


def _merge(oz, cz, gates, wa, wb, bias):
    t, ka = oz.shape
    kb = cz.shape[1]
    d = wa.shape[1]
    nb = d // TN
    return pl.pallas_call(
        _merge_kernel,
        grid=(pl.cdiv(t, TM), nb),
        in_specs=[
            pl.BlockSpec((TM, ka), lambda i, j: (i, 0)),
            pl.BlockSpec((TM, kb), lambda i, j: (i, 0)),
            pl.BlockSpec((TM, TN), lambda i, j: (i, j)),
            pl.BlockSpec((TM, TN), lambda i, j: (i, j + nb)),
            pl.BlockSpec((ka, TN), lambda i, j: (0, j)),
            pl.BlockSpec((kb, TN), lambda i, j: (0, j)),
            pl.BlockSpec((1, TN), lambda i, j: (0, j)),
        ],
        out_specs=pl.BlockSpec((TM, TN), lambda i, j: (i, j)),
        out_shape=jax.ShapeDtypeStruct((t, d), BF16),
        compiler_params=_cp("parallel", "arbitrary"),
        name="merge",
    )(oz, cz, gates, gates, wa, wb, bias.reshape(1, d))


def _out_kernel(m_ref, w_ref, xp_ref, xt_ref, o_ref, *, prompt_blocks):
    acc = jnp.dot(m_ref[...], w_ref[...], preferred_element_type=F32)

    @pl.when(pl.program_id(0) < prompt_blocks)
    def _():
        o_ref[...] = xp_ref[...] + acc

    @pl.when(pl.program_id(0) >= prompt_blocks)
    def _():
        o_ref[...] = xt_ref[...] + acc


def _out_proj(merged, w_out, x_prompt, x_tail):
    t, k = merged.shape
    d = w_out.shape[1]
    nbp = x_prompt.shape[0] // TM
    return pl.pallas_call(
        functools.partial(_out_kernel, prompt_blocks=nbp),
        grid=(pl.cdiv(t, TM), d // TN),
        in_specs=[
            pl.BlockSpec((TM, k), lambda i, j: (i, 0)),
            pl.BlockSpec((k, TN), lambda i, j: (0, j)),
            pl.BlockSpec((TM, TN), lambda i, j: (jnp.minimum(i, nbp - 1), j)),
            pl.BlockSpec((TM, TN), lambda i, j: (jnp.maximum(i - nbp, 0), jnp.where(i >= nbp, j, 0))),
        ],
        out_specs=pl.BlockSpec((TM, TN), lambda i, j: (i, j)),
        out_shape=jax.ShapeDtypeStruct((t, d), F32),
        compiler_params=_cp("parallel", "arbitrary"),
        name="out_proj",
    )(merged, w_out, x_prompt, x_tail)


def _pack_bf16_pairs(x):
    half = x.shape[1] // 2
    bits = lax.bitcast_convert_type(x.astype(BF16).astype(F32), jnp.uint32)
    return (bits[:, :half] >> 16) | (bits[:, half:] & jnp.uint32(0xFFFF0000))


def _unpack_bf16_pairs(w):
    lo = lax.bitcast_convert_type(w << 16, F32).astype(BF16)
    hi = lax.bitcast_convert_type(w & jnp.uint32(0xFFFF0000), F32).astype(BF16)
    return lo, hi


def _router_kernel(h_ref, nw_ref, wr_ref, br_ref, xn_ref, ids_ref, wts_ref):
    x = h_ref[...]
    xn = x * lax.rsqrt(jnp.mean(x * x, axis=-1, keepdims=True) + EPS) * nw_ref[...]
    xn_ref[...] = _pack_bf16_pairs(xn)
    xh, xl = _split_bf16(xn)
    wh, wl = _split_bf16(wr_ref[...])
    dot = lambda a, b: jnp.dot(a, b, preferred_element_type=F32)
    logits = dot(xh, wh) + (dot(xh, wl) + dot(xl, wh)) + br_ref[...]
    lane = lax.broadcasted_iota(jnp.int32, logits.shape, 1)
    neg = -jnp.inf
    big = jnp.int32(1 << 20)
    gl = jnp.where(lane < N_GROUPS, logits, neg)
    gmax = jnp.max(gl, axis=-1, keepdims=True)
    gsum = jnp.sum(jnp.exp(gl - gmax), axis=-1, keepdims=True)
    pg_top = 1.0 / gsum
    gidx = jnp.min(jnp.where(gl == gmax, lane, big), axis=-1, keepdims=True)
    lo = N_GROUPS + gidx * EXPERTS_PER_GROUP
    in_grp = (lane >= lo) & (lane < lo + EXPERTS_PER_GROUP)
    el = jnp.where(in_grp, logits, neg)
    emax = jnp.max(el, axis=-1, keepdims=True)
    ex = jnp.exp(el - emax)
    esum = jnp.sum(ex, axis=-1, keepdims=True)
    pe = ex / esum
    pe = jnp.where(in_grp, pe, -1.0)
    p1 = jnp.max(pe, axis=-1, keepdims=True)
    i1 = jnp.min(jnp.where(pe == p1, lane, big), axis=-1, keepdims=True)
    pe2 = jnp.where(lane == i1, -1.0, pe)
    p2 = jnp.max(pe2, axis=-1, keepdims=True)
    i2 = jnp.min(jnp.where(pe2 == p2, lane, big), axis=-1, keepdims=True)
    denom = p1 + p2
    w1 = pg_top * p1 / denom
    w2 = pg_top * p2 / denom
    ids_ref[...] = jnp.where(lane == 0, i1 - N_GROUPS, jnp.where(lane == 1, i2 - N_GROUPS, 0))
    wts_ref[...] = jnp.where(lane == 0, w1, jnp.where(lane == 1, w2, 0.0))


def _router(h1, norm_w, w_r, b_r, rows=256):
    t, d = h1.shape
    return pl.pallas_call(
        _router_kernel,
        grid=(pl.cdiv(t, rows),),
        in_specs=[
            pl.BlockSpec((rows, d), lambda i: (i, 0)),
            pl.BlockSpec((1, d), lambda i: (0, 0)),
            pl.BlockSpec((d, LANES), lambda i: (0, 0)),
            pl.BlockSpec((1, LANES), lambda i: (0, 0)),
        ],
        out_specs=[
            pl.BlockSpec((rows, d // 2), lambda i: (i, 0)),
            pl.BlockSpec((rows, LANES), lambda i: (i, 0)),
            pl.BlockSpec((rows, LANES), lambda i: (i, 0)),
        ],
        out_shape=[
            jax.ShapeDtypeStruct((t, d // 2), jnp.uint32),
            jax.ShapeDtypeStruct((t, LANES), jnp.int32),
            jax.ShapeDtypeStruct((t, LANES), F32),
        ],
        compiler_params=_cp("parallel"),
        name="router",
    )(h1, norm_w.reshape(1, d), w_r, b_r)


def _experts_kernel(ve_ref, nv_ref, vr_ref, vs_ref, slot_ref, xn_hbm, wg_ref, wu_ref, wd_ref, eo_hbm,
                    xbuf, xlo, xhi, hmid, wdb, acc_ref, gsem, ssem):
    v = pl.program_id(0)
    f = pl.program_id(1)
    nf = pl.num_programs(1)
    nv = nv_ref[0]
    half = xbuf.shape[1]

    def gather_copy(visit, r):
        tok = slot_ref[vs_ref[visit] + r] >> 1
        return pltpu.make_async_copy(xn_hbm.at[pl.ds(tok, 1), :], xbuf.at[pl.ds(r, 1), :], gsem)

    def scatter_copy(visit, r):
        slot = slot_ref[vs_ref[visit] + r]
        return pltpu.make_async_copy(acc_ref.at[pl.ds(r, 1), :], eo_hbm.at[slot & 1, pl.ds(slot >> 1, 1), :], ssem)

    gather_rows_wait = lambda: pltpu.make_async_copy(
        xn_hbm.at[pl.ds(0, DMA_UNROLL), :], xbuf.at[pl.ds(0, DMA_UNROLL), :], gsem).wait()
    scatter_rows_wait = lambda: pltpu.make_async_copy(
        acc_ref.at[pl.ds(0, DMA_UNROLL), :], eo_hbm.at[0, pl.ds(0, DMA_UNROLL), :], ssem).wait()

    def loop(lo, hi, fn):
        def body(i, carry):
            fn(i)
            return carry
        lax.fori_loop(lo, hi, body, 0)

    def unrolled(fn):
        def run(i):
            for j in range(DMA_UNROLL):
                fn(i * DMA_UNROLL + j)
        return run

    nparts = hmid.shape[0]
    unit = DMA_UNROLL
    gather_groups = -(-GROUP_ROWS // (nparts * unit))
    scatter_rows = -(-GROUP_ROWS // ((nparts - 1) * unit)) * unit

    def start_gather(visit, part):
        groups = (vr_ref[visit] + unit - 1) // unit
        lo = part * gather_groups
        loop(lo, jnp.minimum(lo + gather_groups, groups),
             unrolled(lambda r: gather_copy(visit, r).start(priority=ROW_DMA_PRIORITY)))

    def wait_gather(visit):
        loop(0, (vr_ref[visit] + unit - 1) // unit, lambda i: gather_rows_wait())

    def start_scatter(visit, part):
        lo = part * scatter_rows
        hi = jnp.minimum(lo + scatter_rows, vr_ref[visit])
        loop(lo // unit, hi // unit, unrolled(lambda r: scatter_copy(visit, r).start(priority=ROW_DMA_PRIORITY)))
        loop(jnp.maximum((hi // unit) * unit, lo), hi, lambda r: scatter_copy(visit, r).start(priority=ROW_DMA_PRIORITY))

    def wait_scatter(visit):
        n = vr_ref[visit]
        loop(0, n // unit, lambda i: scatter_rows_wait())
        loop((n // unit) * unit, n, lambda r: scatter_copy(visit, r).wait())

    @pl.when((v == 0) & (f == 0))
    def _():
        xbuf[...] = jnp.zeros_like(xbuf)
        for part in range(nparts):
            start_gather(0, part)

    @pl.when((v < nv) & (f == 0))
    def _():
        wait_gather(v)
        lo, hi = _unpack_bf16_pairs(xbuf[...])
        xlo[...] = lo
        xhi[...] = hi

    @pl.when(v + 1 < nv)
    def _():
        start_gather(v + 1, f)

    @pl.when((v > 0) & (v < nv) & (f < nf - 1))
    def _():
        start_scatter(v - 1, f)

    @pl.when(v < nv)
    def _():
        wg = wg_ref[0].astype(BF16)
        wu = wu_ref[0].astype(BF16)
        xa = xlo[...]
        xb = xhi[...]
        g = jnp.dot(xa, wg[:half], preferred_element_type=F32) + jnp.dot(xb, wg[half:], preferred_element_type=F32)
        u = jnp.dot(xa, wu[:half], preferred_element_type=F32) + jnp.dot(xb, wu[half:], preferred_element_type=F32)
        hmid[f] = (_silu(g) * u).astype(BF16)
        wdb[f] = wd_ref[0].astype(BF16)

        @pl.when(f == nf - 1)
        def _():
            @pl.when(v > 0)
            def _():
                wait_scatter(v - 1)
            out = jnp.dot(hmid[0], wdb[0], preferred_element_type=F32)
            for j in range(1, hmid.shape[0]):
                out = out + jnp.dot(hmid[j], wdb[j], preferred_element_type=F32)
            acc_ref[...] = _pack_bf16_pairs(out)

            @pl.when(v == nv - 1)
            def _():
                for part in range(nparts - 1):
                    start_scatter(v, part)
                wait_scatter(v)


def _experts(visit_expert, n_visits, visit_rows, visit_start, row_slot, xn_packed, w_gate, w_up, w_down, max_visits):
    t, half = xn_packed.shape
    d = 2 * half
    fdim = w_gate.shape[2]
    nf = fdim // EXPERT_F_TILE
    ftile = lambda v, f, nv: jnp.where(v < nv[0], f, nf - 1)
    grid_spec = pltpu.PrefetchScalarGridSpec(
        num_scalar_prefetch=5,
        grid=(max_visits, nf),
        in_specs=[
            pl.BlockSpec(memory_space=pl.ANY),
            pl.BlockSpec((1, d, EXPERT_F_TILE), lambda v, f, ve, nv, vr, vs, sl: (ve[v], 0, ftile(v, f, nv))),
            pl.BlockSpec((1, d, EXPERT_F_TILE), lambda v, f, ve, nv, vr, vs, sl: (ve[v], 0, ftile(v, f, nv))),
            pl.BlockSpec((1, EXPERT_F_TILE, d), lambda v, f, ve, nv, vr, vs, sl: (ve[v], ftile(v, f, nv), 0)),
        ],
        out_specs=pl.BlockSpec(memory_space=pl.ANY),
        scratch_shapes=[
            pltpu.VMEM((GROUP_ROWS, half), jnp.uint32),
            pltpu.VMEM((GROUP_ROWS, half), BF16),
            pltpu.VMEM((GROUP_ROWS, half), BF16),
            pltpu.VMEM((nf, GROUP_ROWS, EXPERT_F_TILE), BF16),
            pltpu.VMEM((nf, EXPERT_F_TILE, d), BF16),
            pltpu.VMEM((GROUP_ROWS, half), jnp.uint32),
            pltpu.SemaphoreType.DMA(()),
            pltpu.SemaphoreType.DMA(()),
        ],
    )
    return pl.pallas_call(
        _experts_kernel,
        grid_spec=grid_spec,
        out_shape=jax.ShapeDtypeStruct((2, t, half), jnp.uint32),
        compiler_params=_cp("arbitrary", "arbitrary"),
        name="experts",
    )(visit_expert, n_visits, visit_rows, visit_start, row_slot, xn_packed, w_gate, w_up, w_down)


def _final_kernel(h_ref, o1_ref, o2_ref, wts_ref, nw_ref, y_ref):
    wts = wts_ref[...]
    unpack = lambda w: jnp.concatenate([p.astype(F32) for p in _unpack_bf16_pairs(w)], axis=-1)
    h = h_ref[...] + wts[:, 0:1] * unpack(o1_ref[0]) + wts[:, 1:2] * unpack(o2_ref[0])
    y_ref[...] = h * lax.rsqrt(jnp.mean(h * h, axis=-1, keepdims=True) + EPS) * nw_ref[...]


def _final(h1, eo, wts, norm_w, n_rows, row_block0, rows=FINAL_ROWS):
    d = h1.shape[1]
    rb = lambda i: (i + row_block0, 0)
    return pl.pallas_call(
        _final_kernel,
        grid=(n_rows // rows,),
        in_specs=[
            pl.BlockSpec((rows, d), rb),
            pl.BlockSpec((1, rows, d // 2), lambda i: (0, i + row_block0, 0)),
            pl.BlockSpec((1, rows, d // 2), lambda i: (1, i + row_block0, 0)),
            pl.BlockSpec((rows, LANES), rb),
            pl.BlockSpec((1, d), lambda i: (0, 0)),
        ],
        out_specs=pl.BlockSpec((rows, d), lambda i: (i, 0)),
        out_shape=jax.ShapeDtypeStruct((n_rows, d), F32),
        compiler_params=_cp("parallel"),
        name="final",
    )(h1, eo, eo, wts, norm_w.reshape(1, d))


def _dispatch_plan(ids, max_visits):
    flat_e = ids[:, :2].reshape(-1)
    order = jnp.argsort(flat_e, stable=True).astype(jnp.int32)
    counts = jnp.sum((flat_e[:, None] == jnp.arange(N_EXPERTS, dtype=jnp.int32)[None, :]).astype(jnp.int32), axis=0)
    tiles = (counts + GROUP_ROWS - 1) // GROUP_ROWS
    tile_end = jnp.cumsum(tiles)
    tile_start = tile_end - tiles
    group_start = jnp.cumsum(counts) - counts
    n_visits = tile_end[-1]
    visit = jnp.arange(max_visits, dtype=jnp.int32)
    visit_expert = jnp.sum((tile_end[None, :] <= jnp.minimum(visit, n_visits - 1)[:, None]).astype(jnp.int32), axis=1)
    visit_expert = jnp.minimum(visit_expert, N_EXPERTS - 1)
    tile_in_expert = visit - tile_start[visit_expert]
    rows_left = counts[visit_expert] - GROUP_ROWS * tile_in_expert
    valid = visit < n_visits
    visit_rows = jnp.where(valid, jnp.clip(rows_left, 0, GROUP_ROWS), 0).astype(jnp.int32)
    visit_start = jnp.where(valid, group_start[visit_expert] + GROUP_ROWS * tile_in_expert, 0).astype(jnp.int32)
    row_slot = jnp.pad(order, (0, DMA_UNROLL))
    return row_slot, visit_expert.astype(jnp.int32), n_visits.reshape(1).astype(jnp.int32), visit_rows, visit_start


def kernel(x_prompt, x_sample, state_delta, state_conv_qkv, state_conv_b, meta_tokens,
           norm1, w_in, w_conv_qkv, a_log, dt_bias, w_onorm, w_proj_a, w_dw, b_dw, ln_w, ln_b,
           w_proj_b, b_proj_b, w_out, norm2, w_rg, b_rg, w_re, b_re, w_e_gate, w_e_up, w_e_down,
           final_norm):
    bsz, seq, d = x_prompt.shape
    nsmp, steps, _ = x_sample.shape
    n_meta = meta_tokens.shape[0]
    depth = norm1.shape[0]
    assert depth == 1
    tp = bsz * seq
    ts = nsmp * steps
    t_all = tp + ts + n_meta
    qkv_w = 3 * DN_WIDTH
    conf_w = w_dw.shape[-1]
    hist = CONF_KERNEL - 1

    x_p = x_prompt.reshape(tp, d)
    tail_rows = pl.cdiv(ts + n_meta, TM) * TM
    x_tail = jnp.concatenate([x_sample.transpose(1, 0, 2).reshape(ts, d), meta_tokens,
                              jnp.zeros((tail_rows - ts - n_meta, d), F32)], axis=0)

    wi = w_in[0]
    o_z = qkv_w + DN_WIDTH
    o_glu = o_z + 2 * N_HEADS
    o_gate = o_glu + 2 * conf_w
    w_qkvz = wi[:, :o_z].astype(BF16)
    w_ab = jnp.pad(wi[:, o_z:o_glu], ((0, 0), (0, LANES - 2 * N_HEADS))).astype(BF16)
    w_glu = wi[:, o_glu:o_gate].astype(BF16)
    w_gate = wi[:, o_gate:].astype(BF16)
    alog_pad = jnp.pad(a_log[0], (0, LANES - N_HEADS)).reshape(1, LANES)
    dtb_pad = jnp.pad(dt_bias[0], (0, LANES - N_HEADS)).reshape(1, LANES)

    hn = _rmsnorm_bf16(x_p, x_tail, norm1[0], t_all)
    qkvz = _proj(_mm_plain_kernel, hn, w_qkvz, o_z, F32, "proj_qkvz")
    gbeta = _proj_decay(hn, w_ab, alog_pad, dtb_pad)
    glu = _proj_glu(hn, w_glu)
    gates = _proj(_mm_sigmoid_kernel, hn, w_gate, 2 * d, BF16, "proj_gates")

    wc = w_conv_qkv[0]
    meta_blk = (tp + ts) // n_meta
    qkv_meta = _prep_long(qkvz, wc, n_meta, n_meta, meta_blk, lambda i: 0, True, "prep_meta")
    rows_p = 256
    bps = seq // rows_p
    meta_halo = (tp + ts + n_meta) // 8 - 1
    qkv_p = _prep_long(qkvz, wc, tp, rows_p, 0,
                       lambda i: jnp.where(i % bps == 0, meta_halo, i * (rows_p // 8) - 1), False, "prep_prompt")
    raw_s = qkvz[tp:tp + ts, :qkv_w].reshape(steps, nsmp, qkv_w)
    st_qkv_tm = state_conv_qkv[0].transpose(1, 0, 2)
    qkv_s_tm = _prep_short(raw_s, st_qkv_tm, wc)

    zero_state = jnp.zeros((1, N_HEADS, HEAD_DIM, HEAD_DIM), F32)
    oz_meta, s_meta = _delta(qkv_meta, gbeta, qkvz, 3, zero_state, w_onorm[0], 1, 1, n_meta,
                             lambda n, ci: 0, lambda n, ci: meta_blk, lambda n: 0, "delta_meta")
    n_chunks = seq // CHUNK
    prompt_blk = lambda n, ci: n * n_chunks + ci
    oz_p, s_p = _delta(qkv_p, gbeta, qkvz, 3, s_meta, w_onorm[0], bsz, n_chunks, CHUNK,
                       prompt_blk, prompt_blk, lambda n: 0, "delta_prompt")

    cpad = 8
    to_bm = lambda a: jnp.pad(a.transpose(1, 0, 2), ((0, 0), (0, cpad - steps), (0, 0))).reshape(nsmp * cpad, a.shape[-1])
    qkv_s = to_bm(qkv_s_tm)
    gb_s = to_bm(gbeta[tp:tp + ts].reshape(steps, nsmp, LANES))
    z_s = to_bm(qkvz[tp:tp + ts, qkv_w:].reshape(steps, nsmp, DN_WIDTH))
    oz_s_bm, s_s = _delta(qkv_s, gb_s, z_s, 0, state_delta[0], w_onorm[0], nsmp, 1, cpad,
                          lambda n, ci: n, lambda n, ci: n, lambda n: n, "delta_sample")
    oz_s = oz_s_bm.reshape(nsmp, cpad, DN_WIDTH)[:, :steps].transpose(1, 0, 2).reshape(ts, DN_WIDTH)
    oz = jnp.concatenate([oz_p, oz_s, oz_meta], axis=0)

    zeros32 = jnp.zeros((32, conf_w), F32)
    cz_meta = _conf_long(glu, zeros32, w_dw[0], b_dw[0], ln_w[0], ln_b[0], n_meta, n_meta, meta_blk, 1,
                         lambda i: 0, "conf_meta")
    init_p = jnp.concatenate([jnp.zeros((32 - n_meta, conf_w), F32), glu[tp + ts:]], axis=0)
    rows_c = 128
    cz_p = _conf_long(glu, init_p, w_dw[0], b_dw[0], ln_w[0], ln_b[0], tp, rows_c, 0, seq // rows_c,
                      lambda i: jnp.maximum(i * (rows_c // 32) - 1, 0), "conf_prompt")
    glu_s_tm = glu[tp:tp + ts].reshape(steps, nsmp, conf_w)
    hist_s_tm = state_conv_b[0].transpose(1, 0, 2)
    cz_s = _conf_short(glu_s_tm, hist_s_tm, w_dw[0], b_dw[0], ln_w[0], ln_b[0]).reshape(ts, conf_w)
    cz = jnp.concatenate([cz_p, cz_s, cz_meta], axis=0)

    merged = _merge(oz, cz, gates, w_proj_a[0].astype(BF16), w_proj_b[0].astype(BF16), b_proj_b[0])
    h1 = _out_proj(merged, w_out[0].astype(BF16), x_p, x_tail)

    w_r = jnp.pad(jnp.concatenate([w_rg[0], w_re[0]], axis=1), ((0, 0), (0, LANES - N_GROUPS - N_EXPERTS)))
    b_r = jnp.pad(jnp.concatenate([b_rg[0], b_re[0]]), (0, LANES - N_GROUPS - N_EXPERTS)).reshape(1, LANES)
    xn2, ids, wts = _router(h1, norm2[0], w_r, b_r)
    max_visits = N_EXPERTS + (2 * t_all) // GROUP_ROWS
    row_slot, visit_expert, n_visits, visit_rows, visit_start = _dispatch_plan(ids, max_visits)
    eo = _experts(visit_expert, n_visits, visit_rows, visit_start, row_slot, xn2,
                  w_e_gate[0], w_e_up[0], w_e_down[0], max_visits)
    y_p = _final(h1, eo, wts, final_norm, tp, 0)
    y_s = _final(h1, eo, wts, final_norm, ts, tp // FINAL_ROWS)

    y_prompt = y_p.reshape(bsz, seq, d)
    y_sample = y_s.reshape(steps, nsmp, d).transpose(1, 0, 2)
    new_cq_p = jnp.stack([qkvz[(b + 1) * seq - (SHORT_CONV - 1):(b + 1) * seq, :qkv_w] for b in range(bsz)])
    new_cb_p = jnp.stack([glu[(b + 1) * seq - hist:(b + 1) * seq] for b in range(bsz)])
    cq_s = jnp.concatenate([st_qkv_tm, raw_s], axis=0)[-(SHORT_CONV - 1):].transpose(1, 0, 2)
    cb_s = jnp.concatenate([hist_s_tm, glu_s_tm], axis=0)[-hist:].transpose(1, 0, 2)
    return (y_prompt, y_sample, s_p[None], new_cq_p[None], new_cb_p[None], s_s[None], cq_s[None], cb_s[None])
```

```python
import functools

import jax
import jax.numpy as jnp
from jax import lax
from jax.experimental import pallas as pl
from jax.experimental.pallas import tpu as pltpu

F32 = jnp.float32
BF16 = jnp.bfloat16
EPS = 1e-6

N_HEADS = 16
HEAD_DIM = 128
DN_WIDTH = N_HEADS * HEAD_DIM
SHORT_CONV = 4
CONF_KERNEL = 31
N_GROUPS = 8
EXPERTS_PER_GROUP = 8
N_EXPERTS = N_GROUPS * EXPERTS_PER_GROUP
CHUNK = 64
LANES = 128
HIGHEST = lax.Precision.HIGHEST

TM = 1024
TN = 512
GROUP_ROWS = 320
EXPERT_F_TILE = 256
DMA_UNROLL = 8
ROW_DMA_PRIORITY = 1
CONF_LANE_CHUNK = 512
FINAL_ROWS = 256
VMEM_LIMIT = 56 * 1024 * 1024


def _cp(*sem):
    return pltpu.CompilerParams(dimension_semantics=sem, vmem_limit_bytes=VMEM_LIMIT)


def _sigmoid(x):
    return 1.0 / (1.0 + jnp.exp(-x))


def _silu(x):
    return x * _sigmoid(x)


def _softplus(x):
    return jnp.maximum(x, 0.0) + jnp.log1p(jnp.exp(-jnp.abs(x)))


def _rmsnorm_kernel(xp_ref, xt_ref, w_ref, o_ref, *, prompt_blocks):
    def norm(x):
        ms = jnp.mean(x * x, axis=-1, keepdims=True)
        return (x * lax.rsqrt(ms + EPS) * w_ref[...]).astype(o_ref.dtype)

    @pl.when(pl.program_id(0) < prompt_blocks)
    def _():
        o_ref[...] = norm(xp_ref[...])

    @pl.when(pl.program_id(0) >= prompt_blocks)
    def _():
        o_ref[...] = norm(xt_ref[...])


def _rmsnorm_bf16(x_prompt, x_tail, w, t_all, rows=256):
    tp, d = x_prompt.shape
    nbp = tp // rows
    return pl.pallas_call(
        functools.partial(_rmsnorm_kernel, prompt_blocks=nbp),
        grid=(pl.cdiv(t_all, rows),),
        in_specs=[
            pl.BlockSpec((rows, d), lambda i: (jnp.minimum(i, nbp - 1), 0)),
            pl.BlockSpec((rows, d), lambda i: (jnp.maximum(i - nbp, 0), 0)),
            pl.BlockSpec((1, d), lambda i: (0, 0)),
        ],
        out_specs=pl.BlockSpec((rows, d), lambda i: (i, 0)),
        out_shape=jax.ShapeDtypeStruct((t_all, d), BF16),
        compiler_params=_cp("parallel"),
        name="rmsnorm1",
    )(x_prompt, x_tail, w.reshape(1, d))


def _mm_plain_kernel(x_ref, w_ref, o_ref):
    o_ref[...] = jnp.dot(x_ref[...], w_ref[...], preferred_element_type=F32).astype(o_ref.dtype)


def _mm_sigmoid_kernel(x_ref, w_ref, o_ref):
    o_ref[...] = _sigmoid(jnp.dot(x_ref[...], w_ref[...], preferred_element_type=F32)).astype(o_ref.dtype)


def _mm_glu_kernel(x_ref, wu_ref, wg_ref, o_ref):
    x = x_ref[...]
    u = jnp.dot(x, wu_ref[...], preferred_element_type=F32)
    g = jnp.dot(x, wg_ref[...], preferred_element_type=F32)
    o_ref[...] = u * _sigmoid(g)


def _mm_decay_kernel(x_ref, w_ref, alog_ref, dtb_ref, o_ref):
    acc = jnp.dot(x_ref[...], w_ref[...], preferred_element_type=F32)
    lane = lax.broadcasted_iota(jnp.int32, acc.shape, 1)
    g = -jnp.exp(alog_ref[...]) * _softplus(acc + dtb_ref[...])
    beta = _sigmoid(acc)
    o_ref[...] = jnp.where(lane < N_HEADS, g, jnp.where(lane < 2 * N_HEADS, beta, 0.0))


def _realign_kernel(a_ref, b_ref, o_ref, *, aligned_blocks, shift):
    j = pl.program_id(1)

    @pl.when(j < aligned_blocks)
    def _():
        o_ref[...] = a_ref[...].astype(o_ref.dtype)

    @pl.when(j >= aligned_blocks)
    def _():
        both = jnp.concatenate([a_ref[...], b_ref[...]], axis=1)
        o_ref[...] = both[:, shift:shift + o_ref.shape[1]].astype(o_ref.dtype)


def _realign_w_in(w, aligned_cols, shift, out_cols, rows=1024):
    k, n = w.shape
    return pl.pallas_call(
        functools.partial(_realign_kernel, aligned_blocks=aligned_cols // TN, shift=shift),
        grid=(k // rows, out_cols // TN),
        in_specs=[
            pl.BlockSpec((rows, TN), lambda i, j: (i, j)),
            pl.BlockSpec((rows, LANES), lambda i, j: (i, (j + 1) * (TN // LANES))),
        ],
        out_specs=pl.BlockSpec((rows, TN), lambda i, j: (i, j)),
        out_shape=jax.ShapeDtypeStruct((k, out_cols), BF16),
        compiler_params=_cp("parallel", "parallel"),
        name="realign_w_in",
    )(w, w)


def _proj(kernel_fn, hn, w, col_block0, n_out, out_dtype, name, tn=TN):
    t, k = hn.shape
    return pl.pallas_call(
        kernel_fn,
        grid=(pl.cdiv(t, TM), n_out // tn),
        in_specs=[pl.BlockSpec((TM, k), lambda i, j: (i, 0)), pl.BlockSpec((k, tn), lambda i, j: (0, j + col_block0))],
        out_specs=pl.BlockSpec((TM, tn), lambda i, j: (i, j)),
        out_shape=jax.ShapeDtypeStruct((t, n_out), out_dtype),
        compiler_params=_cp("parallel", "arbitrary"),
        name=name,
    )(hn, w)


def _proj_glu(hn, w, col_block0, half):
    t, k = hn.shape
    nb = half // TN
    return pl.pallas_call(
        _mm_glu_kernel,
        grid=(pl.cdiv(t, TM), nb),
        in_specs=[
            pl.BlockSpec((TM, k), lambda i, j: (i, 0)),
            pl.BlockSpec((k, TN), lambda i, j: (0, j + col_block0)),
            pl.BlockSpec((k, TN), lambda i, j: (0, j + col_block0 + nb)),
        ],
        out_specs=pl.BlockSpec((TM, TN), lambda i, j: (i, j)),
        out_shape=jax.ShapeDtypeStruct((t, half), F32),
        compiler_params=_cp("parallel", "arbitrary"),
        name="proj_glu",
    )(hn, w, w)


def _proj_decay(hn, w_ab, alog_pad, dtb_pad):
    t, k = hn.shape
    return pl.pallas_call(
        _mm_decay_kernel,
        grid=(pl.cdiv(t, TM),),
        in_specs=[
            pl.BlockSpec((TM, k), lambda i: (i, 0)),
            pl.BlockSpec((k, LANES), lambda i: (0, 0)),
            pl.BlockSpec((1, LANES), lambda i: (0, 0)),
            pl.BlockSpec((1, LANES), lambda i: (0, 0)),
        ],
        out_specs=pl.BlockSpec((TM, LANES), lambda i: (i, 0)),
        out_shape=jax.ShapeDtypeStruct((t, LANES), F32),
        compiler_params=_cp("parallel"),
        name="proj_decay",
    )(hn, w_ab, alog_pad, dtb_pad)


def _head_normalize(y, sec):
    outs = []
    for h in range(N_HEADS):
        yh = y[:, h * HEAD_DIM:(h + 1) * HEAD_DIM]
        inv = lax.rsqrt(jnp.sum(yh * yh, axis=-1, keepdims=True) + EPS)
        scale = jnp.where(sec == 0, inv * (HEAD_DIM ** -0.5), jnp.where(sec == 1, inv, 1.0))
        outs.append(yh * scale)
    return jnp.concatenate(outs, axis=-1)


def _prep_long_kernel(cur_ref, halo_ref, w_ref, o_ref, ext_ref, *, rows, zero_halo):
    sec = pl.program_id(1)
    halo = halo_ref[...]
    ext_ref[0:8, :] = jnp.zeros_like(halo) if zero_halo else halo
    ext_ref[8:, :] = cur_ref[...]
    acc = w_ref[SHORT_CONV - 1:SHORT_CONV, :] * cur_ref[...]
    for s in range(1, SHORT_CONV):
        acc = acc + w_ref[SHORT_CONV - 1 - s:SHORT_CONV - s, :] * ext_ref[pl.ds(8 - s, rows), :]
    o_ref[...] = _head_normalize(_silu(acc), sec)


def _prep_long(qkvz, w_conv, n_rows, rows, row_block0, halo_index, zero_halo, name):
    nblk = n_rows // rows
    kern = functools.partial(_prep_long_kernel, rows=rows, zero_halo=zero_halo)
    return pl.pallas_call(
        kern,
        grid=(nblk, 3),
        in_specs=[
            pl.BlockSpec((rows, DN_WIDTH), lambda i, s: (i + row_block0, s)),
            pl.BlockSpec((8, DN_WIDTH), lambda i, s: (halo_index(i), s)),
            pl.BlockSpec((SHORT_CONV, DN_WIDTH), lambda i, s: (0, s)),
        ],
        out_specs=pl.BlockSpec((rows, DN_WIDTH), lambda i, s: (i, s)),
        out_shape=jax.ShapeDtypeStruct((n_rows, 3 * DN_WIDTH), F32),
        scratch_shapes=[pltpu.VMEM((rows + 8, DN_WIDTH), F32)],
        compiler_params=_cp("parallel", "arbitrary"),
        name=name,
    )(qkvz, qkvz, w_conv)


def _prep_short_kernel(x_ref, st_ref, w_ref, o_ref, *, steps):
    sec = pl.program_id(0)
    buf = [st_ref[i] for i in range(SHORT_CONV - 1)] + [x_ref[t] for t in range(steps)]
    for t in range(steps):
        acc = w_ref[0:1, :] * buf[t]
        for i in range(1, SHORT_CONV):
            acc = acc + w_ref[i:i + 1, :] * buf[t + i]
        o_ref[t] = _head_normalize(_silu(acc), sec)


def _prep_short(x_tm, state_tm, w_conv):
    steps, n, _ = x_tm.shape
    kern = functools.partial(_prep_short_kernel, steps=steps)
    return pl.pallas_call(
        kern,
        grid=(3,),
        in_specs=[
            pl.BlockSpec((steps, n, DN_WIDTH), lambda s: (0, 0, s)),
            pl.BlockSpec((SHORT_CONV - 1, n, DN_WIDTH), lambda s: (0, 0, s)),
            pl.BlockSpec((SHORT_CONV, DN_WIDTH), lambda s: (0, s)),
        ],
        out_specs=pl.BlockSpec((steps, n, DN_WIDTH), lambda s: (0, 0, s)),
        out_shape=jax.ShapeDtypeStruct((steps, n, 3 * DN_WIDTH), F32),
        compiler_params=_cp("arbitrary"),
        name="prep_sample",
    )(x_tm, state_tm, w_conv)


def _hdot(a, b):
    return jnp.dot(a, b, preferred_element_type=F32, precision=HIGHEST)


def _split_bf16(a):
    hi = a.astype(BF16)
    lo = (a - hi.astype(F32)).astype(BF16)
    return hi, lo


def _bmm(a, b):
    return jnp.einsum("hik,hkj->hij", a.astype(BF16), b.astype(BF16), preferred_element_type=F32)


def _bmm3(a, b):
    ah, al = _split_bf16(a)
    bh, bl = _split_bf16(b)
    f = lambda x, y: jnp.einsum("hik,hkj->hij", x, y, preferred_element_type=F32)
    return f(ah, bh) + (f(ah, bl) + f(al, bh))


def _unit_lower_inverse(low, c):
    base = min(c, 16)
    row = lax.broadcasted_iota(jnp.int32, (c, c), 0)
    col = lax.broadcasted_iota(jnp.int32, (c, c), 1)
    eye = (row == col).astype(F32)[None]
    diag = jnp.where(((row // base) == (col // base))[None], low, 0.0) if c > base else low
    inv = eye - diag
    power = diag
    k = 2
    while k < base:
        power = _bmm3(power, power)
        inv = inv + _bmm3(inv, power)
        k *= 2
    blk = base
    while blk < c:
        sel = ((row // (2 * blk)) == (col // (2 * blk))) & (((row // blk) % 2) == 1) & (((col // blk) % 2) == 0)
        off = jnp.where(sel[None], low, 0.0)
        inv = inv - _bmm3(inv, _bmm3(off, inv))
        blk *= 2
    return inv


def _delta_kernel(q_ref, k_ref, v_ref, gb_ref, z_ref, s0_ref, wn_ref, o_ref, sout_ref, s_scr, *, c, heads_per_group):
    ci = pl.program_id(1)

    @pl.when(ci == 0)
    def _():
        s_scr[...] = s0_ref[0]

    row = lax.broadcasted_iota(jnp.int32, (c, c), 0)
    col = lax.broadcasted_iota(jnp.int32, (c, c), 1)
    causal = (row >= col)[None]
    strict = (row > col)[None]
    gb = gb_ref[...]
    gcum = _hdot((row >= col).astype(F32), gb)
    gcum_t = gcum.T
    wn = wn_ref[...]
    d = HEAD_DIM
    for h0 in range(0, N_HEADS, heads_per_group):
        heads = range(h0, h0 + heads_per_group)
        hs = slice(h0, h0 + heads_per_group)
        heads_of = lambda ref: jnp.stack([ref[:, h * d:(h + 1) * d] for h in heads])
        q = heads_of(q_ref)
        k = heads_of(k_ref)
        v = heads_of(v_ref)
        g_col = jnp.stack([gcum[:, h:h + 1] for h in heads])
        g_row = jnp.stack([gcum_t[h:h + 1, :] for h in heads])
        beta = jnp.stack([gb[:, N_HEADS + h:N_HEADS + h + 1] for h in heads])
        g_last = g_col[:, c - 1:c, :]
        decay = jnp.where(causal, jnp.exp(g_col - g_row), 0.0)
        exp_g = jnp.exp(g_col)
        kb = k * beta
        kk = jnp.einsum("hid,hjd->hij", jnp.concatenate([kb, q], axis=1).astype(BF16), k.astype(BF16),
                        preferred_element_type=F32)
        lower = jnp.where(strict, kk[:, :c] * decay, 0.0)
        attn = kk[:, c:] * decay
        tinv = _unit_lower_inverse(lower, c)
        rhs = jnp.concatenate([v * beta, kb * exp_g], axis=-1)
        sol = _bmm3(tinv, rhs)
        s = s_scr[hs]
        both = _bmm(jnp.concatenate([sol[:, :, d:], q * exp_g], axis=1), s)
        u = sol[:, :, :d] - both[:, :c]
        o = both[:, c:] + _bmm(attn, u)
        kd = k * jnp.exp(g_last - g_col)
        upd = lax.dot_general(kd.astype(BF16), u.astype(BF16), (((1,), (1,)), ((0,), (0,))),
                              preferred_element_type=F32)
        s_scr[hs] = s * jnp.exp(g_last) + upd
        on = o * lax.rsqrt(jnp.mean(o * o, axis=-1, keepdims=True) + EPS) * wn
        out = (on * _silu(heads_of(z_ref))).astype(o_ref.dtype)
        for i, h in enumerate(heads):
            o_ref[:, h * d:(h + 1) * d] = out[i]

    @pl.when(ci == pl.num_programs(1) - 1)
    def _():
        sout_ref[0] = s_scr[...]


def _delta(qkv, gb, zsrc, z_col_block, s0, w_onorm, n_seq, n_chunks, c, row_block, aux_block, s0_index, name,
           heads_per_group=N_HEADS):
    kern = functools.partial(_delta_kernel, c=c, heads_per_group=heads_per_group)
    rows_out = n_seq * n_chunks * c
    return pl.pallas_call(
        kern,
        grid=(n_seq, n_chunks),
        in_specs=[
            pl.BlockSpec((c, DN_WIDTH), lambda n, ci: (row_block(n, ci), 0)),
            pl.BlockSpec((c, DN_WIDTH), lambda n, ci: (row_block(n, ci), 1)),
            pl.BlockSpec((c, DN_WIDTH), lambda n, ci: (row_block(n, ci), 2)),
            pl.BlockSpec((c, LANES), lambda n, ci: (aux_block(n, ci), 0)),
            pl.BlockSpec((c, DN_WIDTH), lambda n, ci: (aux_block(n, ci), z_col_block)),
            pl.BlockSpec((1, N_HEADS, HEAD_DIM, HEAD_DIM), lambda n, ci: (s0_index(n), 0, 0, 0)),
            pl.BlockSpec((1, HEAD_DIM), lambda n, ci: (0, 0)),
        ],
        out_specs=[
            pl.BlockSpec((c, DN_WIDTH), lambda n, ci: (n * n_chunks + ci, 0)),
            pl.BlockSpec((1, N_HEADS, HEAD_DIM, HEAD_DIM), lambda n, ci: (n, 0, 0, 0)),
        ],
        out_shape=[
            jax.ShapeDtypeStruct((rows_out, DN_WIDTH), BF16),
            jax.ShapeDtypeStruct((n_seq, N_HEADS, HEAD_DIM, HEAD_DIM), F32),
        ],
        scratch_shapes=[pltpu.VMEM((N_HEADS, HEAD_DIM, HEAD_DIM), F32)],
        compiler_params=_cp("parallel", "arbitrary"),
        name=name,
    )(qkv, qkv, qkv, gb, zsrc, s0, w_onorm.reshape(1, HEAD_DIM))


def _layernorm_silu(c, lnw, lnb):
    mu = jnp.mean(c, axis=-1, keepdims=True)
    xc = c - mu
    y = xc * lax.rsqrt(jnp.mean(xc * xc, axis=-1, keepdims=True) + EPS)
    return _silu(y * lnw + lnb)


def _conf_long_kernel(cur_ref, prev_ref, init_ref, w_ref, b_ref, lnw_ref, lnb_ref, o_ref, ext_ref, c_ref, *, rows, blocks_per_seq):
    hist = CONF_KERNEL - 1
    first = (pl.program_id(0) % blocks_per_seq) == 0
    ext_ref[0:32, :] = jnp.where(first, init_ref[...], prev_ref[...])
    ext_ref[32:, :] = cur_ref[...]
    cw = cur_ref.shape[1]
    lc = min(CONF_LANE_CHUNK, cw)
    rowid = lax.broadcasted_iota(jnp.int32, (8, lc), 0)
    for c0 in range(0, cw, lc):
        cs = slice(c0, c0 + lc)

        def qtile(n, r):
            acc = None
            for a in range(5):
                i = 8 * a + r - 2
                if 0 <= i <= hist:
                    term = w_ref[8 * i:8 * i + 8, cs] * ext_ref[8 * (n + a):8 * (n + a) + 8, cs]
                    acc = term if acc is None else acc + term
            return acc

        held = [None] + [qtile(0, r) for r in range(1, 8)]
        for m in range(rows // 8):
            out = qtile(m, 0)
            for r in range(1, 8):
                nxt = qtile(m + 1, r)
                out = out + pltpu.roll(jnp.where(rowid >= r, held[r], nxt), 8 - r, axis=0)
                held[r] = nxt
            c_ref[8 * m:8 * m + 8, cs] = out
    o_ref[...] = _layernorm_silu(c_ref[...] + b_ref[...], lnw_ref[...], lnb_ref[...]).astype(o_ref.dtype)


def _conf_long(glu, init_hist, w_dw, b_dw, ln_w, ln_b, n_rows, rows, row_block0, blocks_per_seq, prev_index, name):
    cw = glu.shape[1]
    nblk = n_rows // rows
    kern = functools.partial(_conf_long_kernel, rows=rows, blocks_per_seq=blocks_per_seq)
    vec = lambda a: a.reshape(1, cw)
    return pl.pallas_call(
        kern,
        grid=(nblk,),
        in_specs=[
            pl.BlockSpec((rows, cw), lambda i: (i + row_block0, 0)),
            pl.BlockSpec((32, cw), lambda i: (prev_index(i), 0)),
            pl.BlockSpec((32, cw), lambda i: (0, 0)),
            pl.BlockSpec((8 * CONF_KERNEL, cw), lambda i: (0, 0)),
            pl.BlockSpec((1, cw), lambda i: (0, 0)),
            pl.BlockSpec((1, cw), lambda i: (0, 0)),
            pl.BlockSpec((1, cw), lambda i: (0, 0)),
        ],
        out_specs=pl.BlockSpec((rows, cw), lambda i: (i, 0)),
        out_shape=jax.ShapeDtypeStruct((n_rows, cw), BF16),
        scratch_shapes=[pltpu.VMEM((rows + 32, cw), F32), pltpu.VMEM((rows, cw), F32)],
        compiler_params=_cp("parallel"),
        name=name,
    )(glu, glu, init_hist, jnp.repeat(w_dw, 8, axis=0), vec(b_dw), vec(ln_w), vec(ln_b))


def _conf_short_kernel(x_ref, hist_ref, w_ref, b_ref, lnw_ref, lnb_ref, o_ref, *, steps):
    nh = CONF_KERNEL - 1
    for t in range(steps):
        acc = jnp.zeros(x_ref.shape[1:], F32)
        for i in range(CONF_KERNEL):
            j = t + i
            src = hist_ref[j] if j < nh else x_ref[j - nh]
            acc = acc + w_ref[i:i + 1, :] * src
        o_ref[t] = _layernorm_silu(acc + b_ref[...], lnw_ref[...], lnb_ref[...]).astype(o_ref.dtype)


def _conf_short(x_tm, hist_tm, w_dw, b_dw, ln_w, ln_b, nb=32):
    steps, n, cw = x_tm.shape
    kern = functools.partial(_conf_short_kernel, steps=steps)
    vec = lambda a: a.reshape(1, cw)
    return pl.pallas_call(
        kern,
        grid=(n // nb,),
        in_specs=[
            pl.BlockSpec((steps, nb, cw), lambda i: (0, i, 0)),
            pl.BlockSpec((CONF_KERNEL - 1, nb, cw), lambda i: (0, i, 0)),
            pl.BlockSpec((CONF_KERNEL, cw), lambda i: (0, 0)),
            pl.BlockSpec((1, cw), lambda i: (0, 0)),
            pl.BlockSpec((1, cw), lambda i: (0, 0)),
            pl.BlockSpec((1, cw), lambda i: (0, 0)),
        ],
        out_specs=pl.BlockSpec((steps, nb, cw), lambda i: (0, i, 0)),
        out_shape=jax.ShapeDtypeStruct((steps, n, cw), BF16),
        compiler_params=_cp("parallel"),
        name="conf_sample",
    )(x_tm, hist_tm, w_dw, vec(b_dw), vec(ln_w), vec(ln_b))


def _merge_kernel(ozp_ref, ozt_ref, czp_ref, czt_ref, ga_ref, gb_ref, wa_ref, wb_ref, bias_ref, o_ref, *, prompt_blocks):
    def body(oz_ref, cz_ref):
        ya = jnp.dot(oz_ref[...], wa_ref[...], preferred_element_type=F32)
        yb = jnp.dot(cz_ref[...], wb_ref[...], preferred_element_type=F32) + bias_ref[...]
        o_ref[...] = (ga_ref[...].astype(F32) * ya + gb_ref[...].astype(F32) * yb).astype(o_ref.dtype)

    @pl.when(pl.program_id(0) < prompt_blocks)
    def _():
        body(ozp_ref, czp_ref)

    @pl.when(pl.program_id(0) >= prompt_blocks)
    def _():
        body(ozt_ref, czt_ref)


def _merge(oz_p, oz_t, cz_p, cz_t, gates, wa, wb, bias):
    t = gates.shape[0]
    ka = oz_p.shape[1]
    kb = cz_p.shape[1]
    d = wa.shape[1]
    nb = d // TN
    nbp = oz_p.shape[0] // TM
    first = lambda i, j: (jnp.minimum(i, nbp - 1), 0)
    rest = lambda i, j: (jnp.maximum(i - nbp, 0), 0)
    return pl.pallas_call(
        functools.partial(_merge_kernel, prompt_blocks=nbp),
        grid=(pl.cdiv(t, TM), nb),
        in_specs=[
            pl.BlockSpec((TM, ka), first),
            pl.BlockSpec((TM, ka), rest),
            pl.BlockSpec((TM, kb), first),
            pl.BlockSpec((TM, kb), rest),
            pl.BlockSpec((TM, TN), lambda i, j: (i, j)),
            pl.BlockSpec((TM, TN), lambda i, j: (i, j + nb)),
            pl.BlockSpec((ka, TN), lambda i, j: (0, j)),
            pl.BlockSpec((kb, TN), lambda i, j: (0, j)),
            pl.BlockSpec((1, TN), lambda i, j: (0, j)),
        ],
        out_specs=pl.BlockSpec((TM, TN), lambda i, j: (i, j)),
        out_shape=jax.ShapeDtypeStruct((t, d), BF16),
        compiler_params=_cp("parallel", "arbitrary"),
        name="merge",
    )(oz_p, oz_t, cz_p, cz_t, gates, gates, wa, wb, bias.reshape(1, d))


def _out_kernel(m_ref, w_ref, xp_ref, xt_ref, o_ref, *, prompt_blocks):
    acc = jnp.dot(m_ref[...], w_ref[...], preferred_element_type=F32)

    @pl.when(pl.program_id(0) < prompt_blocks)
    def _():
        o_ref[...] = xp_ref[...] + acc

    @pl.when(pl.program_id(0) >= prompt_blocks)
    def _():
        o_ref[...] = xt_ref[...] + acc


def _out_proj(merged, w_out, x_prompt, x_tail):
    t, k = merged.shape
    d = w_out.shape[1]
    nbp = x_prompt.shape[0] // TM
    return pl.pallas_call(
        functools.partial(_out_kernel, prompt_blocks=nbp),
        grid=(pl.cdiv(t, TM), d // TN),
        in_specs=[
            pl.BlockSpec((TM, k), lambda i, j: (i, 0)),
            pl.BlockSpec((k, TN), lambda i, j: (0, j)),
            pl.BlockSpec((TM, TN), lambda i, j: (jnp.minimum(i, nbp - 1), j)),
            pl.BlockSpec((TM, TN), lambda i, j: (jnp.maximum(i - nbp, 0), jnp.where(i >= nbp, j, 0))),
        ],
        out_specs=pl.BlockSpec((TM, TN), lambda i, j: (i, j)),
        out_shape=jax.ShapeDtypeStruct((t, d), F32),
        compiler_params=_cp("parallel", "arbitrary"),
        name="out_proj",
    )(merged, w_out, x_prompt, x_tail)


def _pack_bf16_pairs(x):
    half = x.shape[1] // 2
    bits = lax.bitcast_convert_type(x.astype(BF16).astype(F32), jnp.uint32)
    return (bits[:, :half] >> 16) | (bits[:, half:] & jnp.uint32(0xFFFF0000))


def _unpack_bf16_pairs(w):
    lo = lax.bitcast_convert_type(w << 16, F32).astype(BF16)
    hi = lax.bitcast_convert_type(w & jnp.uint32(0xFFFF0000), F32).astype(BF16)
    return lo, hi


def _router_kernel(h_ref, nw_ref, wr_ref, br_ref, xn_ref, ids_ref, wts_ref):
    x = h_ref[...]
    xn = x * lax.rsqrt(jnp.mean(x * x, axis=-1, keepdims=True) + EPS) * nw_ref[...]
    xn_ref[...] = _pack_bf16_pairs(xn)
    xh, xl = _split_bf16(xn)
    wh, wl = _split_bf16(wr_ref[...])
    dot = lambda a, b: jnp.dot(a, b, preferred_element_type=F32)
    logits = dot(xh, wh) + (dot(xh, wl) + dot(xl, wh)) + br_ref[...]
    lane = lax.broadcasted_iota(jnp.int32, logits.shape, 1)
    neg = -jnp.inf
    big = jnp.int32(1 << 20)
    gl = jnp.where(lane < N_GROUPS, logits, neg)
    gmax = jnp.max(gl, axis=-1, keepdims=True)
    gsum = jnp.sum(jnp.exp(gl - gmax), axis=-1, keepdims=True)
    pg_top = 1.0 / gsum
    gidx = jnp.min(jnp.where(gl == gmax, lane, big), axis=-1, keepdims=True)
    lo = N_GROUPS + gidx * EXPERTS_PER_GROUP
    in_grp = (lane >= lo) & (lane < lo + EXPERTS_PER_GROUP)
    el = jnp.where(in_grp, logits, neg)
    emax = jnp.max(el, axis=-1, keepdims=True)
    ex = jnp.exp(el - emax)
    esum = jnp.sum(ex, axis=-1, keepdims=True)
    pe = ex / esum
    pe = jnp.where(in_grp, pe, -1.0)
    p1 = jnp.max(pe, axis=-1, keepdims=True)
    i1 = jnp.min(jnp.where(pe == p1, lane, big), axis=-1, keepdims=True)
    pe2 = jnp.where(lane == i1, -1.0, pe)
    p2 = jnp.max(pe2, axis=-1, keepdims=True)
    i2 = jnp.min(jnp.where(pe2 == p2, lane, big), axis=-1, keepdims=True)
    denom = p1 + p2
    w1 = pg_top * p1 / denom
    w2 = pg_top * p2 / denom
    ids_ref[...] = jnp.where(lane == 0, i1 - N_GROUPS, jnp.where(lane == 1, i2 - N_GROUPS, 0))
    wts_ref[...] = jnp.where(lane == 0, w1, jnp.where(lane == 1, w2, 0.0))


def _router(h1, norm_w, w_r, b_r, rows=256):
    t, d = h1.shape
    return pl.pallas_call(
        _router_kernel,
        grid=(pl.cdiv(t, rows),),
        in_specs=[
            pl.BlockSpec((rows, d), lambda i: (i, 0)),
            pl.BlockSpec((1, d), lambda i: (0, 0)),
            pl.BlockSpec((d, LANES), lambda i: (0, 0)),
            pl.BlockSpec((1, LANES), lambda i: (0, 0)),
        ],
        out_specs=[
            pl.BlockSpec((rows, d // 2), lambda i: (i, 0)),
            pl.BlockSpec((rows, LANES), lambda i: (i, 0)),
            pl.BlockSpec((rows, LANES), lambda i: (i, 0)),
        ],
        out_shape=[
            jax.ShapeDtypeStruct((t, d // 2), jnp.uint32),
            jax.ShapeDtypeStruct((t, LANES), jnp.int32),
            jax.ShapeDtypeStruct((t, LANES), F32),
        ],
        compiler_params=_cp("parallel"),
        name="router",
    )(h1, norm_w.reshape(1, d), w_r, b_r)


def _experts_kernel(ve_ref, nv_ref, vr_ref, vs_ref, slot_ref, xn_hbm, wg_ref, wu_ref, wd_ref, eo_hbm,
                    xbuf, xlo, xhi, hmid, wdb, acc_ref, gsem, ssem):
    v = pl.program_id(0)
    f = pl.program_id(1)
    nf = pl.num_programs(1)
    nv = nv_ref[0]
    half = xbuf.shape[1]

    def gather_copy(visit, r):
        tok = slot_ref[vs_ref[visit] + r] >> 1
        return pltpu.make_async_copy(xn_hbm.at[pl.ds(tok, 1), :], xbuf.at[pl.ds(r, 1), :], gsem)

    def scatter_copy(visit, r):
        slot = slot_ref[vs_ref[visit] + r]
        return pltpu.make_async_copy(acc_ref.at[pl.ds(r, 1), :], eo_hbm.at[slot & 1, pl.ds(slot >> 1, 1), :], ssem)

    gather_rows_wait = lambda: pltpu.make_async_copy(
        xn_hbm.at[pl.ds(0, DMA_UNROLL), :], xbuf.at[pl.ds(0, DMA_UNROLL), :], gsem).wait()
    scatter_rows_wait = lambda: pltpu.make_async_copy(
        acc_ref.at[pl.ds(0, DMA_UNROLL), :], eo_hbm.at[0, pl.ds(0, DMA_UNROLL), :], ssem).wait()

    def loop(lo, hi, fn):
        def body(i, carry):
            fn(i)
            return carry
        lax.fori_loop(lo, hi, body, 0)

    def unrolled(fn):
        def run(i):
            for j in range(DMA_UNROLL):
                fn(i * DMA_UNROLL + j)
        return run

    nparts = hmid.shape[0]
    unit = DMA_UNROLL
    gather_groups = -(-GROUP_ROWS // (nparts * unit))
    scatter_rows = -(-GROUP_ROWS // ((nparts - 1) * unit)) * unit

    def start_gather(visit, part):
        groups = (vr_ref[visit] + unit - 1) // unit
        lo = part * gather_groups
        loop(lo, jnp.minimum(lo + gather_groups, groups),
             unrolled(lambda r: gather_copy(visit, r).start(priority=ROW_DMA_PRIORITY)))

    def wait_gather(visit):
        loop(0, (vr_ref[visit] + unit - 1) // unit, lambda i: gather_rows_wait())

    def start_scatter(visit, part):
        lo = part * scatter_rows
        hi = jnp.minimum(lo + scatter_rows, vr_ref[visit])
        loop(lo // unit, hi // unit, unrolled(lambda r: scatter_copy(visit, r).start(priority=ROW_DMA_PRIORITY)))
        loop(jnp.maximum((hi // unit) * unit, lo), hi, lambda r: scatter_copy(visit, r).start(priority=ROW_DMA_PRIORITY))

    def wait_scatter(visit):
        n = vr_ref[visit]
        loop(0, n // unit, lambda i: scatter_rows_wait())
        loop((n // unit) * unit, n, lambda r: scatter_copy(visit, r).wait())

    @pl.when((v == 0) & (f == 0))
    def _():
        xbuf[...] = jnp.zeros_like(xbuf)
        for part in range(nparts):
            start_gather(0, part)

    @pl.when((v < nv) & (f == 0))
    def _():
        wait_gather(v)
        lo, hi = _unpack_bf16_pairs(xbuf[...])
        xlo[...] = lo
        xhi[...] = hi

    @pl.when(v + 1 < nv)
    def _():
        start_gather(v + 1, f)

    @pl.when((v > 0) & (v < nv) & (f < nf - 1))
    def _():
        start_scatter(v - 1, f)

    @pl.when(v < nv)
    def _():
        wg = wg_ref[0].astype(BF16)
        wu = wu_ref[0].astype(BF16)
        xa = xlo[...]
        xb = xhi[...]
        g = jnp.dot(xa, wg[:half], preferred_element_type=F32) + jnp.dot(xb, wg[half:], preferred_element_type=F32)
        u = jnp.dot(xa, wu[:half], preferred_element_type=F32) + jnp.dot(xb, wu[half:], preferred_element_type=F32)
        hmid[f] = (_silu(g) * u).astype(BF16)
        wdb[f] = wd_ref[0].astype(BF16)

        @pl.when(f == nf - 1)
        def _():
            @pl.when(v > 0)
            def _():
                wait_scatter(v - 1)
            out = jnp.dot(hmid[0], wdb[0], preferred_element_type=F32)
            for j in range(1, hmid.shape[0]):
                out = out + jnp.dot(hmid[j], wdb[j], preferred_element_type=F32)
            acc_ref[...] = _pack_bf16_pairs(out)

            @pl.when(v == nv - 1)
            def _():
                for part in range(nparts - 1):
                    start_scatter(v, part)
                wait_scatter(v)


def _experts(visit_expert, n_visits, visit_rows, visit_start, row_slot, xn_packed, w_gate, w_up, w_down, max_visits):
    t, half = xn_packed.shape
    d = 2 * half
    fdim = w_gate.shape[2]
    nf = fdim // EXPERT_F_TILE
    ftile = lambda v, f, nv: jnp.where(v < nv[0], f, nf - 1)
    grid_spec = pltpu.PrefetchScalarGridSpec(
        num_scalar_prefetch=5,
        grid=(max_visits, nf),
        in_specs=[
            pl.BlockSpec(memory_space=pl.ANY),
            pl.BlockSpec((1, d, EXPERT_F_TILE), lambda v, f, ve, nv, vr, vs, sl: (ve[v], 0, ftile(v, f, nv))),
            pl.BlockSpec((1, d, EXPERT_F_TILE), lambda v, f, ve, nv, vr, vs, sl: (ve[v], 0, ftile(v, f, nv))),
            pl.BlockSpec((1, EXPERT_F_TILE, d), lambda v, f, ve, nv, vr, vs, sl: (ve[v], ftile(v, f, nv), 0)),
        ],
        out_specs=pl.BlockSpec(memory_space=pl.ANY),
        scratch_shapes=[
            pltpu.VMEM((GROUP_ROWS, half), jnp.uint32),
            pltpu.VMEM((GROUP_ROWS, half), BF16),
            pltpu.VMEM((GROUP_ROWS, half), BF16),
            pltpu.VMEM((nf, GROUP_ROWS, EXPERT_F_TILE), BF16),
            pltpu.VMEM((nf, EXPERT_F_TILE, d), BF16),
            pltpu.VMEM((GROUP_ROWS, half), jnp.uint32),
            pltpu.SemaphoreType.DMA(()),
            pltpu.SemaphoreType.DMA(()),
        ],
    )
    return pl.pallas_call(
        _experts_kernel,
        grid_spec=grid_spec,
        out_shape=jax.ShapeDtypeStruct((2, t, half), jnp.uint32),
        compiler_params=_cp("arbitrary", "arbitrary"),
        name="experts",
    )(visit_expert, n_visits, visit_rows, visit_start, row_slot, xn_packed, w_gate, w_up, w_down)


def _final_kernel(h_ref, o1_ref, o2_ref, wts_ref, nw_ref, y_ref):
    wts = wts_ref[...]
    unpack = lambda w: jnp.concatenate([p.astype(F32) for p in _unpack_bf16_pairs(w)], axis=-1)
    h = h_ref[...] + wts[:, 0:1] * unpack(o1_ref[0]) + wts[:, 1:2] * unpack(o2_ref[0])
    y_ref[...] = h * lax.rsqrt(jnp.mean(h * h, axis=-1, keepdims=True) + EPS) * nw_ref[...]


def _final(h1, eo, wts, norm_w, n_rows, row_block0, rows=FINAL_ROWS):
    d = h1.shape[1]
    rb = lambda i: (i + row_block0, 0)
    return pl.pallas_call(
        _final_kernel,
        grid=(n_rows // rows,),
        in_specs=[
            pl.BlockSpec((rows, d), rb),
            pl.BlockSpec((1, rows, d // 2), lambda i: (0, i + row_block0, 0)),
            pl.BlockSpec((1, rows, d // 2), lambda i: (1, i + row_block0, 0)),
            pl.BlockSpec((rows, LANES), rb),
            pl.BlockSpec((1, d), lambda i: (0, 0)),
        ],
        out_specs=pl.BlockSpec((rows, d), lambda i: (i, 0)),
        out_shape=jax.ShapeDtypeStruct((n_rows, d), F32),
        compiler_params=_cp("parallel"),
        name="final",
    )(h1, eo, eo, wts, norm_w.reshape(1, d))


def _dispatch_plan(ids, max_visits):
    flat_e = ids[:, :2].reshape(-1)
    order = jnp.argsort(flat_e, stable=True).astype(jnp.int32)
    counts = jnp.sum((flat_e[:, None] == jnp.arange(N_EXPERTS, dtype=jnp.int32)[None, :]).astype(jnp.int32), axis=0)
    tiles = (counts + GROUP_ROWS - 1) // GROUP_ROWS
    tile_end = jnp.cumsum(tiles)
    tile_start = tile_end - tiles
    group_start = jnp.cumsum(counts) - counts
    n_visits = tile_end[-1]
    visit = jnp.arange(max_visits, dtype=jnp.int32)
    visit_expert = jnp.sum((tile_end[None, :] <= jnp.minimum(visit, n_visits - 1)[:, None]).astype(jnp.int32), axis=1)
    visit_expert = jnp.minimum(visit_expert, N_EXPERTS - 1)
    tile_in_expert = visit - tile_start[visit_expert]
    rows_left = counts[visit_expert] - GROUP_ROWS * tile_in_expert
    valid = visit < n_visits
    visit_rows = jnp.where(valid, jnp.clip(rows_left, 0, GROUP_ROWS), 0).astype(jnp.int32)
    visit_start = jnp.where(valid, group_start[visit_expert] + GROUP_ROWS * tile_in_expert, 0).astype(jnp.int32)
    row_slot = jnp.pad(order, (0, DMA_UNROLL))
    return row_slot, visit_expert.astype(jnp.int32), n_visits.reshape(1).astype(jnp.int32), visit_rows, visit_start


def kernel(x_prompt, x_sample, state_delta, state_conv_qkv, state_conv_b, meta_tokens,
           norm1, w_in, w_conv_qkv, a_log, dt_bias, w_onorm, w_proj_a, w_dw, b_dw, ln_w, ln_b,
           w_proj_b, b_proj_b, w_out, norm2, w_rg, b_rg, w_re, b_re, w_e_gate, w_e_up, w_e_down,
           final_norm):
    bsz, seq, d = x_prompt.shape
    nsmp, steps, _ = x_sample.shape
    n_meta = meta_tokens.shape[0]
    depth = norm1.shape[0]
    assert depth == 1
    tp = bsz * seq
    ts = nsmp * steps
    t_all = tp + ts + n_meta
    qkv_w = 3 * DN_WIDTH
    conf_w = w_dw.shape[-1]
    hist = CONF_KERNEL - 1

    x_p = x_prompt.reshape(tp, d)
    tail_rows = pl.cdiv(ts + n_meta, TM) * TM
    x_tail = jnp.concatenate([x_sample.transpose(1, 0, 2).reshape(ts, d), meta_tokens,
                              jnp.zeros((tail_rows - ts - n_meta, d), F32)], axis=0)

    wi = w_in[0]
    o_z = qkv_w + DN_WIDTH
    o_glu = o_z + 2 * N_HEADS
    w_all = _realign_w_in(wi, o_z, o_glu - o_z, o_z + 2 * conf_w + 2 * d)
    w_ab = jnp.pad(wi[:, o_z:o_glu], ((0, 0), (0, LANES - 2 * N_HEADS))).astype(BF16)
    alog_pad = jnp.pad(a_log[0], (0, LANES - N_HEADS)).reshape(1, LANES)
    dtb_pad = jnp.pad(dt_bias[0], (0, LANES - N_HEADS)).reshape(1, LANES)

    hn = _rmsnorm_bf16(x_p, x_tail, norm1[0], t_all)
    qkvz = _proj(_mm_plain_kernel, hn, w_all, 0, o_z, F32, "proj_qkvz")
    gbeta = _proj_decay(hn, w_ab, alog_pad, dtb_pad)
    glu = _proj_glu(hn, w_all, o_z // TN, conf_w)
    gates = _proj(_mm_sigmoid_kernel, hn, w_all, (o_z + 2 * conf_w) // TN, 2 * d, BF16, "proj_gates")

    wc = w_conv_qkv[0]
    meta_blk = (tp + ts) // n_meta
    qkv_meta = _prep_long(qkvz, wc, n_meta, n_meta, meta_blk, lambda i: 0, True, "prep_meta")
    rows_p = 256
    bps = seq // rows_p
    meta_halo = (tp + ts + n_meta) // 8 - 1
    qkv_p = _prep_long(qkvz, wc, tp, rows_p, 0,
                       lambda i: jnp.where(i % bps == 0, meta_halo, i * (rows_p // 8) - 1), False, "prep_prompt")
    raw_s = qkvz[tp:tp + ts, :qkv_w].reshape(steps, nsmp, qkv_w)
    st_qkv_tm = state_conv_qkv[0].transpose(1, 0, 2)
    qkv_s_tm = _prep_short(raw_s, st_qkv_tm, wc)

    zero_state = jnp.zeros((1, N_HEADS, HEAD_DIM, HEAD_DIM), F32)
    oz_meta, s_meta = _delta(qkv_meta, gbeta, qkvz, 3, zero_state, w_onorm[0], 1, 1, n_meta,
                             lambda n, ci: 0, lambda n, ci: meta_blk, lambda n: 0, "delta_meta")
    n_chunks = seq // CHUNK
    prompt_blk = lambda n, ci: n * n_chunks + ci
    oz_p, s_p = _delta(qkv_p, gbeta, qkvz, 3, s_meta, w_onorm[0], bsz, n_chunks, CHUNK,
                       prompt_blk, prompt_blk, lambda n: 0, "delta_prompt")

    cpad = 8
    to_bm = lambda a: jnp.pad(a.transpose(1, 0, 2), ((0, 0), (0, cpad - steps), (0, 0))).reshape(nsmp * cpad, a.shape[-1])
    qkv_s = to_bm(qkv_s_tm)
    gb_s = to_bm(gbeta[tp:tp + ts].reshape(steps, nsmp, LANES))
    z_s = to_bm(qkvz[tp:tp + ts, qkv_w:].reshape(steps, nsmp, DN_WIDTH))
    oz_s_bm, s_s = _delta(qkv_s, gb_s, z_s, 0, state_delta[0], w_onorm[0], nsmp, 1, cpad,
                          lambda n, ci: n, lambda n, ci: n, lambda n: n, "delta_sample")
    oz_s = oz_s_bm.reshape(nsmp, cpad, DN_WIDTH)[:, :steps].transpose(1, 0, 2).reshape(ts, DN_WIDTH)
    tail_pad = lambda w: jnp.zeros((tail_rows - ts - n_meta, w), BF16)
    oz_t = jnp.concatenate([oz_s, oz_meta, tail_pad(DN_WIDTH)], axis=0)

    zeros32 = jnp.zeros((32, conf_w), F32)
    cz_meta = _conf_long(glu, zeros32, w_dw[0], b_dw[0], ln_w[0], ln_b[0], n_meta, n_meta, meta_blk, 1,
                         lambda i: 0, "conf_meta")
    init_p = jnp.concatenate([jnp.zeros((32 - n_meta, conf_w), F32), glu[tp + ts:]], axis=0)
    rows_c = 128
    cz_p = _conf_long(glu, init_p, w_dw[0], b_dw[0], ln_w[0], ln_b[0], tp, rows_c, 0, seq // rows_c,
                      lambda i: jnp.maximum(i * (rows_c // 32) - 1, 0), "conf_prompt")
    glu_s_tm = glu[tp:tp + ts].reshape(steps, nsmp, conf_w)
    hist_s_tm = state_conv_b[0].transpose(1, 0, 2)
    cz_s = _conf_short(glu_s_tm, hist_s_tm, w_dw[0], b_dw[0], ln_w[0], ln_b[0]).reshape(ts, conf_w)
    cz_t = jnp.concatenate([cz_s, cz_meta, tail_pad(conf_w)], axis=0)

    merged = _merge(oz_p, oz_t, cz_p, cz_t, gates, w_proj_a[0].astype(BF16), w_proj_b[0].astype(BF16), b_proj_b[0])
    h1 = _out_proj(merged, w_out[0].astype(BF16), x_p, x_tail)

    w_r = jnp.pad(jnp.concatenate([w_rg[0], w_re[0]], axis=1), ((0, 0), (0, LANES - N_GROUPS - N_EXPERTS)))
    b_r = jnp.pad(jnp.concatenate([b_rg[0], b_re[0]]), (0, LANES - N_GROUPS - N_EXPERTS)).reshape(1, LANES)
    xn2, ids, wts = _router(h1, norm2[0], w_r, b_r)
    max_visits = N_EXPERTS + (2 * t_all) // GROUP_ROWS
    row_slot, visit_expert, n_visits, visit_rows, visit_start = _dispatch_plan(ids, max_visits)
    eo = _experts(visit_expert, n_visits, visit_rows, visit_start, row_slot, xn2,
                  w_e_gate[0], w_e_up[0], w_e_down[0], max_visits)
    y_p = _final(h1, eo, wts, final_norm, tp, 0)
    y_s = _final(h1, eo, wts, final_norm, ts, tp // FINAL_ROWS)

    y_prompt = y_p.reshape(bsz, seq, d)
    y_sample = y_s.reshape(steps, nsmp, d).transpose(1, 0, 2)
    new_cq_p = jnp.stack([qkvz[(b + 1) * seq - (SHORT_CONV - 1):(b + 1) * seq, :qkv_w] for b in range(bsz)])
    new_cb_p = jnp.stack([glu[(b + 1) * seq - hist:(b + 1) * seq] for b in range(bsz)])
    cq_s = jnp.concatenate([st_qkv_tm, raw_s], axis=0)[-(SHORT_CONV - 1):].transpose(1, 0, 2)
    cb_s = jnp.concatenate([hist_s_tm, glu_s_tm], axis=0)[-hist:].transpose(1, 0, 2)
    return (y_prompt, y_sample, s_p[None], new_cq_p[None], new_cb_p[None], s_s[None], cq_s[None], cb_s[None])
```

```python
import functools

import jax
import jax.numpy as jnp
from jax import lax
from jax.experimental import pallas as pl
from jax.experimental.pallas import tpu as pltpu

F32 = jnp.float32
BF16 = jnp.bfloat16
EPS = 1e-6

N_HEADS = 16
HEAD_DIM = 128
DN_WIDTH = N_HEADS * HEAD_DIM
SHORT_CONV = 4
CONF_KERNEL = 31
N_GROUPS = 8
EXPERTS_PER_GROUP = 8
N_EXPERTS = N_GROUPS * EXPERTS_PER_GROUP
CHUNK = 64
LANES = 128
HIGHEST = lax.Precision.HIGHEST

TM = 1024
TN = 512
GROUP_ROWS = 320
EXPERT_F_TILE = 256
DMA_UNROLL = 8
ROW_DMA_PRIORITY = 1
CONF_LANE_CHUNK = 512
FINAL_ROWS = 256
VMEM_LIMIT = 56 * 1024 * 1024


def _cp(*sem):
    return pltpu.CompilerParams(dimension_semantics=sem, vmem_limit_bytes=VMEM_LIMIT)


def _sigmoid(x):
    return 1.0 / (1.0 + jnp.exp(-x))


def _silu(x):
    return x * _sigmoid(x)


def _softplus(x):
    return jnp.maximum(x, 0.0) + jnp.log1p(jnp.exp(-jnp.abs(x)))


def _rmsnorm_kernel(xp_ref, xt_ref, w_ref, o_ref, *, prompt_blocks):
    def norm(x):
        ms = jnp.mean(x * x, axis=-1, keepdims=True)
        return (x * lax.rsqrt(ms + EPS) * w_ref[...]).astype(o_ref.dtype)

    @pl.when(pl.program_id(0) < prompt_blocks)
    def _():
        o_ref[...] = norm(xp_ref[...])

    @pl.when(pl.program_id(0) >= prompt_blocks)
    def _():
        o_ref[...] = norm(xt_ref[...])


def _rmsnorm_bf16(x_prompt, x_tail, w, t_all, rows=256):
    tp, d = x_prompt.shape
    nbp = tp // rows
    return pl.pallas_call(
        functools.partial(_rmsnorm_kernel, prompt_blocks=nbp),
        grid=(pl.cdiv(t_all, rows),),
        in_specs=[
            pl.BlockSpec((rows, d), lambda i: (jnp.minimum(i, nbp - 1), 0)),
            pl.BlockSpec((rows, d), lambda i: (jnp.maximum(i - nbp, 0), 0)),
            pl.BlockSpec((1, d), lambda i: (0, 0)),
        ],
        out_specs=pl.BlockSpec((rows, d), lambda i: (i, 0)),
        out_shape=jax.ShapeDtypeStruct((t_all, d), BF16),
        compiler_params=_cp("parallel"),
        name="rmsnorm1",
    )(x_prompt, x_tail, w.reshape(1, d))


def _mm_plain_kernel(x_ref, w_ref, o_ref):
    o_ref[...] = jnp.dot(x_ref[...], w_ref[...], preferred_element_type=F32).astype(o_ref.dtype)


def _mm_sigmoid_kernel(x_ref, w_ref, o_ref):
    o_ref[...] = _sigmoid(jnp.dot(x_ref[...], w_ref[...], preferred_element_type=F32)).astype(o_ref.dtype)


def _mm_glu_kernel(x_ref, wu_ref, wg_ref, o_ref):
    x = x_ref[...]
    u = jnp.dot(x, wu_ref[...], preferred_element_type=F32)
    g = jnp.dot(x, wg_ref[...], preferred_element_type=F32)
    o_ref[...] = u * _sigmoid(g)


def _mm_decay_kernel(x_ref, w_ref, alog_ref, dtb_ref, o_ref):
    acc = lax.dot_general(x_ref[...], w_ref[...].astype(BF16), (((1,), (1,)), ((), ())), preferred_element_type=F32)
    lane = lax.broadcasted_iota(jnp.int32, acc.shape, 1)
    g = -jnp.exp(alog_ref[...]) * _softplus(acc + dtb_ref[...])
    beta = _sigmoid(acc)
    o_ref[...] = jnp.where(lane < N_HEADS, g, jnp.where(lane < 2 * N_HEADS, beta, 0.0))


def _realign_kernel(a_ref, b_ref, o_ref, *, aligned_blocks, shift):
    j = pl.program_id(0)

    @pl.when(j < aligned_blocks)
    def _():
        o_ref[...] = a_ref[...].T.astype(o_ref.dtype)

    @pl.when(j >= aligned_blocks)
    def _():
        rows = jnp.concatenate([a_ref[shift:, :], b_ref[...]], axis=0)
        o_ref[...] = rows.T.astype(o_ref.dtype)


def _realign_w_in(wt, aligned_cols, shift, out_cols):
    n, k = wt.shape
    return pl.pallas_call(
        functools.partial(_realign_kernel, aligned_blocks=aligned_cols // TN, shift=shift),
        grid=(out_cols // TN,),
        in_specs=[
            pl.BlockSpec((TN, k), lambda j: (j, 0)),
            pl.BlockSpec((shift, k), lambda j: ((j + 1) * (TN // shift), 0)),
        ],
        out_specs=pl.BlockSpec((k, TN), lambda j: (0, j)),
        out_shape=jax.ShapeDtypeStruct((k, out_cols), BF16),
        compiler_params=_cp("parallel"),
        name="realign_w_in",
    )(wt, wt)


def _proj(kernel_fn, hn, w, col_block0, n_out, out_dtype, name, tn=TN):
    t, k = hn.shape
    return pl.pallas_call(
        kernel_fn,
        grid=(pl.cdiv(t, TM), n_out // tn),
        in_specs=[pl.BlockSpec((TM, k), lambda i, j: (i, 0)), pl.BlockSpec((k, tn), lambda i, j: (0, j + col_block0))],
        out_specs=pl.BlockSpec((TM, tn), lambda i, j: (i, j)),
        out_shape=jax.ShapeDtypeStruct((t, n_out), out_dtype),
        compiler_params=_cp("parallel", "arbitrary"),
        name=name,
    )(hn, w)


def _proj_glu(hn, w, col_block0, half):
    t, k = hn.shape
    nb = half // TN
    return pl.pallas_call(
        _mm_glu_kernel,
        grid=(pl.cdiv(t, TM), nb),
        in_specs=[
            pl.BlockSpec((TM, k), lambda i, j: (i, 0)),
            pl.BlockSpec((k, TN), lambda i, j: (0, j + col_block0)),
            pl.BlockSpec((k, TN), lambda i, j: (0, j + col_block0 + nb)),
        ],
        out_specs=pl.BlockSpec((TM, TN), lambda i, j: (i, j)),
        out_shape=jax.ShapeDtypeStruct((t, half), F32),
        compiler_params=_cp("parallel", "arbitrary"),
        name="proj_glu",
    )(hn, w, w)


def _proj_decay(hn, wt, row0, alog_pad, dtb_pad):
    t, k = hn.shape
    return pl.pallas_call(
        _mm_decay_kernel,
        grid=(pl.cdiv(t, TM),),
        in_specs=[
            pl.BlockSpec((TM, k), lambda i: (i, 0)),
            pl.BlockSpec((LANES, k), lambda i: (row0 // LANES, 0)),
            pl.BlockSpec((1, LANES), lambda i: (0, 0)),
            pl.BlockSpec((1, LANES), lambda i: (0, 0)),
        ],
        out_specs=pl.BlockSpec((TM, LANES), lambda i: (i, 0)),
        out_shape=jax.ShapeDtypeStruct((t, LANES), F32),
        compiler_params=_cp("parallel"),
        name="proj_decay",
    )(hn, wt, alog_pad, dtb_pad)


def _head_normalize(y, sec):
    outs = []
    for h in range(N_HEADS):
        yh = y[:, h * HEAD_DIM:(h + 1) * HEAD_DIM]
        inv = lax.rsqrt(jnp.sum(yh * yh, axis=-1, keepdims=True) + EPS)
        scale = jnp.where(sec == 0, inv * (HEAD_DIM ** -0.5), jnp.where(sec == 1, inv, 1.0))
        outs.append(yh * scale)
    return jnp.concatenate(outs, axis=-1)


def _prep_long_kernel(cur_ref, halo_ref, w_ref, o_ref, ext_ref, *, rows, zero_halo):
    sec = pl.program_id(1)
    halo = halo_ref[...]
    ext_ref[0:8, :] = jnp.zeros_like(halo) if zero_halo else halo
    ext_ref[8:, :] = cur_ref[...]
    acc = w_ref[SHORT_CONV - 1:SHORT_CONV, :] * cur_ref[...]
    for s in range(1, SHORT_CONV):
        acc = acc + w_ref[SHORT_CONV - 1 - s:SHORT_CONV - s, :] * ext_ref[pl.ds(8 - s, rows), :]
    o_ref[...] = _head_normalize(_silu(acc), sec)


def _prep_long(qkvz, w_conv, n_rows, rows, row_block0, halo_index, zero_halo, name):
    nblk = n_rows // rows
    kern = functools.partial(_prep_long_kernel, rows=rows, zero_halo=zero_halo)
    return pl.pallas_call(
        kern,
        grid=(nblk, 3),
        in_specs=[
            pl.BlockSpec((rows, DN_WIDTH), lambda i, s: (i + row_block0, s)),
            pl.BlockSpec((8, DN_WIDTH), lambda i, s: (halo_index(i), s)),
            pl.BlockSpec((SHORT_CONV, DN_WIDTH), lambda i, s: (0, s)),
        ],
        out_specs=pl.BlockSpec((rows, DN_WIDTH), lambda i, s: (i, s)),
        out_shape=jax.ShapeDtypeStruct((n_rows, 3 * DN_WIDTH), F32),
        scratch_shapes=[pltpu.VMEM((rows + 8, DN_WIDTH), F32)],
        compiler_params=_cp("parallel", "arbitrary"),
        name=name,
    )(qkvz, qkvz, w_conv)


def _prep_short_kernel(x_ref, st_ref, w_ref, o_ref, *, steps):
    sec = pl.program_id(0)
    buf = [st_ref[i] for i in range(SHORT_CONV - 1)] + [x_ref[t] for t in range(steps)]
    for t in range(steps):
        acc = w_ref[0:1, :] * buf[t]
        for i in range(1, SHORT_CONV):
            acc = acc + w_ref[i:i + 1, :] * buf[t + i]
        o_ref[t] = _head_normalize(_silu(acc), sec)


def _prep_short(x_tm, state_tm, w_conv):
    steps, n, _ = x_tm.shape
    kern = functools.partial(_prep_short_kernel, steps=steps)
    return pl.pallas_call(
        kern,
        grid=(3,),
        in_specs=[
            pl.BlockSpec((steps, n, DN_WIDTH), lambda s: (0, 0, s)),
            pl.BlockSpec((SHORT_CONV - 1, n, DN_WIDTH), lambda s: (0, 0, s)),
            pl.BlockSpec((SHORT_CONV, DN_WIDTH), lambda s: (0, s)),
        ],
        out_specs=pl.BlockSpec((steps, n, DN_WIDTH), lambda s: (0, 0, s)),
        out_shape=jax.ShapeDtypeStruct((steps, n, 3 * DN_WIDTH), F32),
        compiler_params=_cp("arbitrary"),
        name="prep_sample",
    )(x_tm, state_tm, w_conv)


def _hdot(a, b):
    return jnp.dot(a, b, preferred_element_type=F32, precision=HIGHEST)


def _split_bf16(a):
    hi = a.astype(BF16)
    lo = (a - hi.astype(F32)).astype(BF16)
    return hi, lo


def _bmm(a, b):
    return jnp.einsum("hik,hkj->hij", a.astype(BF16), b.astype(BF16), preferred_element_type=F32)


def _bmm3(a, b):
    ah, al = _split_bf16(a)
    bh, bl = _split_bf16(b)
    f = lambda x, y: jnp.einsum("hik,hkj->hij", x, y, preferred_element_type=F32)
    return f(ah, bh) + (f(ah, bl) + f(al, bh))


def _unit_lower_inverse(low, c):
    base = min(c, 16)
    row = lax.broadcasted_iota(jnp.int32, (c, c), 0)
    col = lax.broadcasted_iota(jnp.int32, (c, c), 1)
    eye = (row == col).astype(F32)[None]
    diag = jnp.where(((row // base) == (col // base))[None], low, 0.0) if c > base else low
    inv = eye - diag
    power = diag
    k = 2
    while k < base:
        power = _bmm3(power, power)
        inv = inv + _bmm3(inv, power)
        k *= 2
    blk = base
    while blk < c:
        sel = ((row // (2 * blk)) == (col // (2 * blk))) & (((row // blk) % 2) == 1) & (((col // blk) % 2) == 0)
        off = jnp.where(sel[None], low, 0.0)
        inv = inv - _bmm3(inv, _bmm3(off, inv))
        blk *= 2
    return inv


def _delta_kernel(q_ref, k_ref, v_ref, gb_ref, z_ref, s0_ref, wn_ref, o_ref, sout_ref, s_scr, *, c, heads_per_group):
    ci = pl.program_id(1)

    @pl.when(ci == 0)
    def _():
        s_scr[...] = s0_ref[0]

    row = lax.broadcasted_iota(jnp.int32, (c, c), 0)
    col = lax.broadcasted_iota(jnp.int32, (c, c), 1)
    causal = (row >= col)[None]
    strict = (row > col)[None]
    gb = gb_ref[...]
    gcum = _hdot((row >= col).astype(F32), gb)
    gcum_t = gcum.T
    wn = wn_ref[...]
    d = HEAD_DIM
    for h0 in range(0, N_HEADS, heads_per_group):
        heads = range(h0, h0 + heads_per_group)
        hs = slice(h0, h0 + heads_per_group)
        heads_of = lambda ref: jnp.stack([ref[:, h * d:(h + 1) * d] for h in heads])
        q = heads_of(q_ref)
        k = heads_of(k_ref)
        v = heads_of(v_ref)
        g_col = jnp.stack([gcum[:, h:h + 1] for h in heads])
        g_row = jnp.stack([gcum_t[h:h + 1, :] for h in heads])
        beta = jnp.stack([gb[:, N_HEADS + h:N_HEADS + h + 1] for h in heads])
        g_last = g_col[:, c - 1:c, :]
        decay = jnp.where(causal, jnp.exp(g_col - g_row), 0.0)
        exp_g = jnp.exp(g_col)
        kb = k * beta
        kk = jnp.einsum("hid,hjd->hij", jnp.concatenate([kb, q], axis=1).astype(BF16), k.astype(BF16),
                        preferred_element_type=F32)
        lower = jnp.where(strict, kk[:, :c] * decay, 0.0)
        attn = kk[:, c:] * decay
        tinv = _unit_lower_inverse(lower, c)
        rhs = jnp.concatenate([v * beta, kb * exp_g], axis=-1)
        sol = _bmm3(tinv, rhs)
        s = s_scr[hs]
        both = _bmm(jnp.concatenate([sol[:, :, d:], q * exp_g], axis=1), s)
        u = sol[:, :, :d] - both[:, :c]
        o = both[:, c:] + _bmm(attn, u)
        kd = k * jnp.exp(g_last - g_col)
        upd = lax.dot_general(kd.astype(BF16), u.astype(BF16), (((1,), (1,)), ((0,), (0,))),
                              preferred_element_type=F32)
        s_scr[hs] = s * jnp.exp(g_last) + upd
        on = o * lax.rsqrt(jnp.mean(o * o, axis=-1, keepdims=True) + EPS) * wn
        out = (on * _silu(heads_of(z_ref))).astype(o_ref.dtype)
        for i, h in enumerate(heads):
            o_ref[:, h * d:(h + 1) * d] = out[i]

    @pl.when(ci == pl.num_programs(1) - 1)
    def _():
        sout_ref[0] = s_scr[...]


def _delta(qkv, gb, zsrc, z_col_block, s0, w_onorm, n_seq, n_chunks, c, row_block, aux_block, s0_index, name,
           heads_per_group=N_HEADS):
    kern = functools.partial(_delta_kernel, c=c, heads_per_group=heads_per_group)
    rows_out = n_seq * n_chunks * c
    return pl.pallas_call(
        kern,
        grid=(n_seq, n_chunks),
        in_specs=[
            pl.BlockSpec((c, DN_WIDTH), lambda n, ci: (row_block(n, ci), 0)),
            pl.BlockSpec((c, DN_WIDTH), lambda n, ci: (row_block(n, ci), 1)),
            pl.BlockSpec((c, DN_WIDTH), lambda n, ci: (row_block(n, ci), 2)),
            pl.BlockSpec((c, LANES), lambda n, ci: (aux_block(n, ci), 0)),
            pl.BlockSpec((c, DN_WIDTH), lambda n, ci: (aux_block(n, ci), z_col_block)),
            pl.BlockSpec((1, N_HEADS, HEAD_DIM, HEAD_DIM), lambda n, ci: (s0_index(n), 0, 0, 0)),
            pl.BlockSpec((1, HEAD_DIM), lambda n, ci: (0, 0)),
        ],
        out_specs=[
            pl.BlockSpec((c, DN_WIDTH), lambda n, ci: (n * n_chunks + ci, 0)),
            pl.BlockSpec((1, N_HEADS, HEAD_DIM, HEAD_DIM), lambda n, ci: (n, 0, 0, 0)),
        ],
        out_shape=[
            jax.ShapeDtypeStruct((rows_out, DN_WIDTH), BF16),
            jax.ShapeDtypeStruct((n_seq, N_HEADS, HEAD_DIM, HEAD_DIM), F32),
        ],
        scratch_shapes=[pltpu.VMEM((N_HEADS, HEAD_DIM, HEAD_DIM), F32)],
        compiler_params=_cp("parallel", "arbitrary"),
        name=name,
    )(qkv, qkv, qkv, gb, zsrc, s0, w_onorm.reshape(1, HEAD_DIM))


def _layernorm_silu(c, lnw, lnb):
    mu = jnp.mean(c, axis=-1, keepdims=True)
    xc = c - mu
    y = xc * lax.rsqrt(jnp.mean(xc * xc, axis=-1, keepdims=True) + EPS)
    return _silu(y * lnw + lnb)


def _conf_long_kernel(cur_ref, prev_ref, init_ref, w_ref, b_ref, lnw_ref, lnb_ref, o_ref, ext_ref, c_ref, *, rows, blocks_per_seq):
    hist = CONF_KERNEL - 1
    first = (pl.program_id(0) % blocks_per_seq) == 0
    ext_ref[0:32, :] = jnp.where(first, init_ref[...], prev_ref[...])
    ext_ref[32:, :] = cur_ref[...]
    cw = cur_ref.shape[1]
    lc = min(CONF_LANE_CHUNK, cw)
    rowid = lax.broadcasted_iota(jnp.int32, (8, lc), 0)
    for c0 in range(0, cw, lc):
        cs = slice(c0, c0 + lc)

        def qtile(n, r):
            acc = None
            for a in range(5):
                i = 8 * a + r - 2
                if 0 <= i <= hist:
                    term = w_ref[8 * i:8 * i + 8, cs] * ext_ref[8 * (n + a):8 * (n + a) + 8, cs]
                    acc = term if acc is None else acc + term
            return acc

        held = [None] + [qtile(0, r) for r in range(1, 8)]
        for m in range(rows // 8):
            out = qtile(m, 0)
            for r in range(1, 8):
                nxt = qtile(m + 1, r)
                out = out + pltpu.roll(jnp.where(rowid >= r, held[r], nxt), 8 - r, axis=0)
                held[r] = nxt
            c_ref[8 * m:8 * m + 8, cs] = out
    o_ref[...] = _layernorm_silu(c_ref[...] + b_ref[...], lnw_ref[...], lnb_ref[...]).astype(o_ref.dtype)


def _conf_long(glu, init_hist, w_dw, b_dw, ln_w, ln_b, n_rows, rows, row_block0, blocks_per_seq, prev_index, name):
    cw = glu.shape[1]
    nblk = n_rows // rows
    kern = functools.partial(_conf_long_kernel, rows=rows, blocks_per_seq=blocks_per_seq)
    vec = lambda a: a.reshape(1, cw)
    return pl.pallas_call(
        kern,
        grid=(nblk,),
        in_specs=[
            pl.BlockSpec((rows, cw), lambda i: (i + row_block0, 0)),
            pl.BlockSpec((32, cw), lambda i: (prev_index(i), 0)),
            pl.BlockSpec((32, cw), lambda i: (0, 0)),
            pl.BlockSpec((8 * CONF_KERNEL, cw), lambda i: (0, 0)),
            pl.BlockSpec((1, cw), lambda i: (0, 0)),
            pl.BlockSpec((1, cw), lambda i: (0, 0)),
            pl.BlockSpec((1, cw), lambda i: (0, 0)),
        ],
        out_specs=pl.BlockSpec((rows, cw), lambda i: (i, 0)),
        out_shape=jax.ShapeDtypeStruct((n_rows, cw), BF16),
        scratch_shapes=[pltpu.VMEM((rows + 32, cw), F32), pltpu.VMEM((rows, cw), F32)],
        compiler_params=_cp("parallel"),
        name=name,
    )(glu, glu, init_hist, jnp.repeat(w_dw, 8, axis=0), vec(b_dw), vec(ln_w), vec(ln_b))


def _conf_short_kernel(x_ref, hist_ref, w_ref, b_ref, lnw_ref, lnb_ref, o_ref, *, steps):
    nh = CONF_KERNEL - 1
    for t in range(steps):
        acc = jnp.zeros(x_ref.shape[1:], F32)
        for i in range(CONF_KERNEL):
            j = t + i
            src = hist_ref[j] if j < nh else x_ref[j - nh]
            acc = acc + w_ref[i:i + 1, :] * src
        o_ref[t] = _layernorm_silu(acc + b_ref[...], lnw_ref[...], lnb_ref[...]).astype(o_ref.dtype)


def _conf_short(x_tm, hist_tm, w_dw, b_dw, ln_w, ln_b, nb=32):
    steps, n, cw = x_tm.shape
    kern = functools.partial(_conf_short_kernel, steps=steps)
    vec = lambda a: a.reshape(1, cw)
    return pl.pallas_call(
        kern,
        grid=(n // nb,),
        in_specs=[
            pl.BlockSpec((steps, nb, cw), lambda i: (0, i, 0)),
            pl.BlockSpec((CONF_KERNEL - 1, nb, cw), lambda i: (0, i, 0)),
            pl.BlockSpec((CONF_KERNEL, cw), lambda i: (0, 0)),
            pl.BlockSpec((1, cw), lambda i: (0, 0)),
            pl.BlockSpec((1, cw), lambda i: (0, 0)),
            pl.BlockSpec((1, cw), lambda i: (0, 0)),
        ],
        out_specs=pl.BlockSpec((steps, nb, cw), lambda i: (0, i, 0)),
        out_shape=jax.ShapeDtypeStruct((steps, n, cw), BF16),
        compiler_params=_cp("parallel"),
        name="conf_sample",
    )(x_tm, hist_tm, w_dw, vec(b_dw), vec(ln_w), vec(ln_b))


def _merge_kernel(ozp_ref, ozt_ref, czp_ref, czt_ref, ga_ref, gb_ref, wa_ref, wb_ref, bias_ref, o_ref, *, prompt_blocks):
    def body(oz_ref, cz_ref):
        ya = jnp.dot(oz_ref[...], wa_ref[...], preferred_element_type=F32)
        yb = jnp.dot(cz_ref[...], wb_ref[...], preferred_element_type=F32) + bias_ref[...]
        o_ref[...] = (ga_ref[...].astype(F32) * ya + gb_ref[...].astype(F32) * yb).astype(o_ref.dtype)

    @pl.when(pl.program_id(0) < prompt_blocks)
    def _():
        body(ozp_ref, czp_ref)

    @pl.when(pl.program_id(0) >= prompt_blocks)
    def _():
        body(ozt_ref, czt_ref)


def _merge(oz_p, oz_t, cz_p, cz_t, gates, wa, wb, bias):
    t = gates.shape[0]
    ka = oz_p.shape[1]
    kb = cz_p.shape[1]
    d = wa.shape[1]
    nb = d // TN
    nbp = oz_p.shape[0] // TM
    first = lambda i, j: (jnp.minimum(i, nbp - 1), 0)
    rest = lambda i, j: (jnp.maximum(i - nbp, 0), 0)
    return pl.pallas_call(
        functools.partial(_merge_kernel, prompt_blocks=nbp),
        grid=(pl.cdiv(t, TM), nb),
        in_specs=[
            pl.BlockSpec((TM, ka), first),
            pl.BlockSpec((TM, ka), rest),
            pl.BlockSpec((TM, kb), first),
            pl.BlockSpec((TM, kb), rest),
            pl.BlockSpec((TM, TN), lambda i, j: (i, j)),
            pl.BlockSpec((TM, TN), lambda i, j: (i, j + nb)),
            pl.BlockSpec((ka, TN), lambda i, j: (0, j)),
            pl.BlockSpec((kb, TN), lambda i, j: (0, j)),
            pl.BlockSpec((1, TN), lambda i, j: (0, j)),
        ],
        out_specs=pl.BlockSpec((TM, TN), lambda i, j: (i, j)),
        out_shape=jax.ShapeDtypeStruct((t, d), BF16),
        compiler_params=_cp("parallel", "arbitrary"),
        name="merge",
    )(oz_p, oz_t, cz_p, cz_t, gates, gates, wa, wb, bias.reshape(1, d))


def _out_kernel(m_ref, w_ref, xp_ref, xt_ref, o_ref, *, prompt_blocks):
    acc = jnp.dot(m_ref[...], w_ref[...], preferred_element_type=F32)

    @pl.when(pl.program_id(0) < prompt_blocks)
    def _():
        o_ref[...] = xp_ref[...] + acc

    @pl.when(pl.program_id(0) >= prompt_blocks)
    def _():
        o_ref[...] = xt_ref[...] + acc


def _out_proj(merged, w_out, x_prompt, x_tail):
    t, k = merged.shape
    d = w_out.shape[1]
    nbp = x_prompt.shape[0] // TM
    return pl.pallas_call(
        functools.partial(_out_kernel, prompt_blocks=nbp),
        grid=(pl.cdiv(t, TM), d // TN),
        in_specs=[
            pl.BlockSpec((TM, k), lambda i, j: (i, 0)),
            pl.BlockSpec((k, TN), lambda i, j: (0, j)),
            pl.BlockSpec((TM, TN), lambda i, j: (jnp.minimum(i, nbp - 1), j)),
            pl.BlockSpec((TM, TN), lambda i, j: (jnp.maximum(i - nbp, 0), jnp.where(i >= nbp, j, 0))),
        ],
        out_specs=pl.BlockSpec((TM, TN), lambda i, j: (i, j)),
        out_shape=jax.ShapeDtypeStruct((t, d), F32),
        compiler_params=_cp("parallel", "arbitrary"),
        name="out_proj",
    )(merged, w_out, x_prompt, x_tail)


def _pack_bf16_pairs(x):
    half = x.shape[1] // 2
    bits = lax.bitcast_convert_type(x.astype(BF16).astype(F32), jnp.uint32)
    return (bits[:, :half] >> 16) | (bits[:, half:] & jnp.uint32(0xFFFF0000))


def _unpack_bf16_pairs(w):
    lo = lax.bitcast_convert_type(w << 16, F32).astype(BF16)
    hi = lax.bitcast_convert_type(w & jnp.uint32(0xFFFF0000), F32).astype(BF16)
    return lo, hi


def _router_kernel(h_ref, nw_ref, wr_ref, br_ref, xn_ref, ids_ref, wts_ref):
    x = h_ref[...]
    xn = x * lax.rsqrt(jnp.mean(x * x, axis=-1, keepdims=True) + EPS) * nw_ref[...]
    xn_ref[...] = _pack_bf16_pairs(xn)
    xh, xl = _split_bf16(xn)
    wh, wl = _split_bf16(wr_ref[...])
    dot = lambda a, b: jnp.dot(a, b, preferred_element_type=F32)
    logits = dot(xh, wh) + (dot(xh, wl) + dot(xl, wh)) + br_ref[...]
    lane = lax.broadcasted_iota(jnp.int32, logits.shape, 1)
    neg = -jnp.inf
    big = jnp.int32(1 << 20)
    gl = jnp.where(lane < N_GROUPS, logits, neg)
    gmax = jnp.max(gl, axis=-1, keepdims=True)
    gsum = jnp.sum(jnp.exp(gl - gmax), axis=-1, keepdims=True)
    pg_top = 1.0 / gsum
    gidx = jnp.min(jnp.where(gl == gmax, lane, big), axis=-1, keepdims=True)
    lo = N_GROUPS + gidx * EXPERTS_PER_GROUP
    in_grp = (lane >= lo) & (lane < lo + EXPERTS_PER_GROUP)
    el = jnp.where(in_grp, logits, neg)
    emax = jnp.max(el, axis=-1, keepdims=True)
    ex = jnp.exp(el - emax)
    esum = jnp.sum(ex, axis=-1, keepdims=True)
    pe = ex / esum
    pe = jnp.where(in_grp, pe, -1.0)
    p1 = jnp.max(pe, axis=-1, keepdims=True)
    i1 = jnp.min(jnp.where(pe == p1, lane, big), axis=-1, keepdims=True)
    pe2 = jnp.where(lane == i1, -1.0, pe)
    p2 = jnp.max(pe2, axis=-1, keepdims=True)
    i2 = jnp.min(jnp.where(pe2 == p2, lane, big), axis=-1, keepdims=True)
    denom = p1 + p2
    w1 = pg_top * p1 / denom
    w2 = pg_top * p2 / denom
    ids_ref[...] = jnp.where(lane == 0, i1 - N_GROUPS, jnp.where(lane == 1, i2 - N_GROUPS, 0))
    wts_ref[...] = jnp.where(lane == 0, w1, jnp.where(lane == 1, w2, 0.0))


def _router(h1, norm_w, w_r, b_r, rows=256):
    t, d = h1.shape
    return pl.pallas_call(
        _router_kernel,
        grid=(pl.cdiv(t, rows),),
        in_specs=[
            pl.BlockSpec((rows, d), lambda i: (i, 0)),
            pl.BlockSpec((1, d), lambda i: (0, 0)),
            pl.BlockSpec((d, LANES), lambda i: (0, 0)),
            pl.BlockSpec((1, LANES), lambda i: (0, 0)),
        ],
        out_specs=[
            pl.BlockSpec((rows, d // 2), lambda i: (i, 0)),
            pl.BlockSpec((rows, LANES), lambda i: (i, 0)),
            pl.BlockSpec((rows, LANES), lambda i: (i, 0)),
        ],
        out_shape=[
            jax.ShapeDtypeStruct((t, d // 2), jnp.uint32),
            jax.ShapeDtypeStruct((t, LANES), jnp.int32),
            jax.ShapeDtypeStruct((t, LANES), F32),
        ],
        compiler_params=_cp("parallel"),
        name="router",
    )(h1, norm_w.reshape(1, d), w_r, b_r)


def _experts_kernel(ve_ref, nv_ref, vr_ref, vs_ref, slot_ref, xn_hbm, wg_ref, wu_ref, wd_ref, eo_hbm,
                    xbuf, xlo, xhi, hmid, wdb, acc_ref, gsem, ssem):
    v = pl.program_id(0)
    f = pl.program_id(1)
    nf = pl.num_programs(1)
    nv = nv_ref[0]
    half = xbuf.shape[1]

    def gather_copy(visit, r):
        tok = slot_ref[vs_ref[visit] + r] >> 1
        return pltpu.make_async_copy(xn_hbm.at[pl.ds(tok, 1), :], xbuf.at[pl.ds(r, 1), :], gsem)

    def scatter_copy(visit, r):
        slot = slot_ref[vs_ref[visit] + r]
        return pltpu.make_async_copy(acc_ref.at[pl.ds(r, 1), :], eo_hbm.at[slot & 1, pl.ds(slot >> 1, 1), :], ssem)

    gather_rows_wait = lambda: pltpu.make_async_copy(
        xn_hbm.at[pl.ds(0, DMA_UNROLL), :], xbuf.at[pl.ds(0, DMA_UNROLL), :], gsem).wait()
    scatter_rows_wait = lambda: pltpu.make_async_copy(
        acc_ref.at[pl.ds(0, DMA_UNROLL), :], eo_hbm.at[0, pl.ds(0, DMA_UNROLL), :], ssem).wait()

    def loop(lo, hi, fn):
        def body(i, carry):
            fn(i)
            return carry
        lax.fori_loop(lo, hi, body, 0)

    def unrolled(fn):
        def run(i):
            for j in range(DMA_UNROLL):
                fn(i * DMA_UNROLL + j)
        return run

    nparts = hmid.shape[0]
    unit = DMA_UNROLL
    gather_groups = -(-GROUP_ROWS // (nparts * unit))
    scatter_rows = -(-GROUP_ROWS // ((nparts - 1) * unit)) * unit

    def start_gather(visit, part):
        groups = (vr_ref[visit] + unit - 1) // unit
        lo = part * gather_groups
        loop(lo, jnp.minimum(lo + gather_groups, groups),
             unrolled(lambda r: gather_copy(visit, r).start(priority=ROW_DMA_PRIORITY)))

    def wait_gather(visit):
        loop(0, (vr_ref[visit] + unit - 1) // unit, lambda i: gather_rows_wait())

    def start_scatter(visit, part):
        lo = part * scatter_rows
        hi = jnp.minimum(lo + scatter_rows, vr_ref[visit])
        loop(lo // unit, hi // unit, unrolled(lambda r: scatter_copy(visit, r).start(priority=ROW_DMA_PRIORITY)))
        loop(jnp.maximum((hi // unit) * unit, lo), hi, lambda r: scatter_copy(visit, r).start(priority=ROW_DMA_PRIORITY))

    def wait_scatter(visit):
        n = vr_ref[visit]
        loop(0, n // unit, lambda i: scatter_rows_wait())
        loop((n // unit) * unit, n, lambda r: scatter_copy(visit, r).wait())

    @pl.when((v == 0) & (f == 0))
    def _():
        xbuf[...] = jnp.zeros_like(xbuf)
        for part in range(nparts):
            start_gather(0, part)

    @pl.when((v < nv) & (f == 0))
    def _():
        wait_gather(v)
        lo, hi = _unpack_bf16_pairs(xbuf[...])
        xlo[...] = lo
        xhi[...] = hi

    @pl.when(v + 1 < nv)
    def _():
        start_gather(v + 1, f)

    @pl.when((v > 0) & (v < nv) & (f < nf - 1))
    def _():
        start_scatter(v - 1, f)

    @pl.when(v < nv)
    def _():
        wg = wg_ref[0].astype(BF16)
        wu = wu_ref[0].astype(BF16)
        xa = xlo[...]
        xb = xhi[...]
        g = jnp.dot(xa, wg[:half], preferred_element_type=F32) + jnp.dot(xb, wg[half:], preferred_element_type=F32)
        u = jnp.dot(xa, wu[:half], preferred_element_type=F32) + jnp.dot(xb, wu[half:], preferred_element_type=F32)
        hmid[f] = (_silu(g) * u).astype(BF16)
        wdb[f] = wd_ref[0].astype(BF16)

        @pl.when(f == nf - 1)
        def _():
            @pl.when(v > 0)
            def _():
                wait_scatter(v - 1)
            out = jnp.dot(hmid[0], wdb[0], preferred_element_type=F32)
            for j in range(1, hmid.shape[0]):
                out = out + jnp.dot(hmid[j], wdb[j], preferred_element_type=F32)
            acc_ref[...] = _pack_bf16_pairs(out)

            @pl.when(v == nv - 1)
            def _():
                for part in range(nparts - 1):
                    start_scatter(v, part)
                wait_scatter(v)


def _experts(visit_expert, n_visits, visit_rows, visit_start, row_slot, xn_packed, w_gate, w_up, w_down, max_visits):
    t, half = xn_packed.shape
    d = 2 * half
    fdim = w_gate.shape[2]
    nf = fdim // EXPERT_F_TILE
    ftile = lambda v, f, nv: jnp.where(v < nv[0], f, nf - 1)
    grid_spec = pltpu.PrefetchScalarGridSpec(
        num_scalar_prefetch=5,
        grid=(max_visits, nf),
        in_specs=[
            pl.BlockSpec(memory_space=pl.ANY),
            pl.BlockSpec((1, d, EXPERT_F_TILE), lambda v, f, ve, nv, vr, vs, sl: (ve[v], 0, ftile(v, f, nv))),
            pl.BlockSpec((1, d, EXPERT_F_TILE), lambda v, f, ve, nv, vr, vs, sl: (ve[v], 0, ftile(v, f, nv))),
            pl.BlockSpec((1, EXPERT_F_TILE, d), lambda v, f, ve, nv, vr, vs, sl: (ve[v], ftile(v, f, nv), 0)),
        ],
        out_specs=pl.BlockSpec(memory_space=pl.ANY),
        scratch_shapes=[
            pltpu.VMEM((GROUP_ROWS, half), jnp.uint32),
            pltpu.VMEM((GROUP_ROWS, half), BF16),
            pltpu.VMEM((GROUP_ROWS, half), BF16),
            pltpu.VMEM((nf, GROUP_ROWS, EXPERT_F_TILE), BF16),
            pltpu.VMEM((nf, EXPERT_F_TILE, d), BF16),
            pltpu.VMEM((GROUP_ROWS, half), jnp.uint32),
            pltpu.SemaphoreType.DMA(()),
            pltpu.SemaphoreType.DMA(()),
        ],
    )
    return pl.pallas_call(
        _experts_kernel,
        grid_spec=grid_spec,
        out_shape=jax.ShapeDtypeStruct((2, t, half), jnp.uint32),
        compiler_params=_cp("arbitrary", "arbitrary"),
        name="experts",
    )(visit_expert, n_visits, visit_rows, visit_start, row_slot, xn_packed, w_gate, w_up, w_down)


def _final_kernel(h_ref, o1_ref, o2_ref, wts_ref, nw_ref, y_ref):
    wts = wts_ref[...]
    unpack = lambda w: jnp.concatenate([p.astype(F32) for p in _unpack_bf16_pairs(w)], axis=-1)
    h = h_ref[...] + wts[:, 0:1] * unpack(o1_ref[0]) + wts[:, 1:2] * unpack(o2_ref[0])
    y_ref[...] = h * lax.rsqrt(jnp.mean(h * h, axis=-1, keepdims=True) + EPS) * nw_ref[...]


def _final(h1, eo, wts, norm_w, n_rows, row_block0, rows=FINAL_ROWS):
    d = h1.shape[1]
    rb = lambda i: (i + row_block0, 0)
    return pl.pallas_call(
        _final_kernel,
        grid=(n_rows // rows,),
        in_specs=[
            pl.BlockSpec((rows, d), rb),
            pl.BlockSpec((1, rows, d // 2), lambda i: (0, i + row_block0, 0)),
            pl.BlockSpec((1, rows, d // 2), lambda i: (1, i + row_block0, 0)),
            pl.BlockSpec((rows, LANES), rb),
            pl.BlockSpec((1, d), lambda i: (0, 0)),
        ],
        out_specs=pl.BlockSpec((rows, d), lambda i: (i, 0)),
        out_shape=jax.ShapeDtypeStruct((n_rows, d), F32),
        compiler_params=_cp("parallel"),
        name="final",
    )(h1, eo, eo, wts, norm_w.reshape(1, d))


def _dispatch_plan(ids, max_visits):
    flat_e = ids[:, :2].reshape(-1)
    order = jnp.argsort(flat_e, stable=True).astype(jnp.int32)
    counts = jnp.sum((flat_e[:, None] == jnp.arange(N_EXPERTS, dtype=jnp.int32)[None, :]).astype(jnp.int32), axis=0)
    tiles = (counts + GROUP_ROWS - 1) // GROUP_ROWS
    tile_end = jnp.cumsum(tiles)
    tile_start = tile_end - tiles
    group_start = jnp.cumsum(counts) - counts
    n_visits = tile_end[-1]
    visit = jnp.arange(max_visits, dtype=jnp.int32)
    visit_expert = jnp.sum((tile_end[None, :] <= jnp.minimum(visit, n_visits - 1)[:, None]).astype(jnp.int32), axis=1)
    visit_expert = jnp.minimum(visit_expert, N_EXPERTS - 1)
    tile_in_expert = visit - tile_start[visit_expert]
    rows_left = counts[visit_expert] - GROUP_ROWS * tile_in_expert
    valid = visit < n_visits
    visit_rows = jnp.where(valid, jnp.clip(rows_left, 0, GROUP_ROWS), 0).astype(jnp.int32)
    visit_start = jnp.where(valid, group_start[visit_expert] + GROUP_ROWS * tile_in_expert, 0).astype(jnp.int32)
    row_slot = jnp.pad(order, (0, DMA_UNROLL))
    return row_slot, visit_expert.astype(jnp.int32), n_visits.reshape(1).astype(jnp.int32), visit_rows, visit_start


def kernel(x_prompt, x_sample, state_delta, state_conv_qkv, state_conv_b, meta_tokens,
           norm1, w_in, w_conv_qkv, a_log, dt_bias, w_onorm, w_proj_a, w_dw, b_dw, ln_w, ln_b,
           w_proj_b, b_proj_b, w_out, norm2, w_rg, b_rg, w_re, b_re, w_e_gate, w_e_up, w_e_down,
           final_norm):
    bsz, seq, d = x_prompt.shape
    nsmp, steps, _ = x_sample.shape
    n_meta = meta_tokens.shape[0]
    depth = norm1.shape[0]
    assert depth == 1
    tp = bsz * seq
    ts = nsmp * steps
    t_all = tp + ts + n_meta
    qkv_w = 3 * DN_WIDTH
    conf_w = w_dw.shape[-1]
    hist = CONF_KERNEL - 1

    x_p = x_prompt.reshape(tp, d)
    tail_rows = pl.cdiv(ts + n_meta, TM) * TM
    x_tail = jnp.concatenate([x_sample.transpose(1, 0, 2).reshape(ts, d), meta_tokens,
                              jnp.zeros((tail_rows - ts - n_meta, d), F32)], axis=0)

    wt = jnp.transpose(w_in[0])
    o_z = qkv_w + DN_WIDTH
    o_glu = o_z + 2 * N_HEADS
    w_all = _realign_w_in(wt, o_z, o_glu - o_z, o_z + 2 * conf_w + 2 * d)
    alog_pad = jnp.pad(a_log[0], (0, LANES - N_HEADS)).reshape(1, LANES)
    dtb_pad = jnp.pad(dt_bias[0], (0, LANES - N_HEADS)).reshape(1, LANES)

    hn = _rmsnorm_bf16(x_p, x_tail, norm1[0], t_all)
    qkvz = _proj(_mm_plain_kernel, hn, w_all, 0, o_z, F32, "proj_qkvz")
    gbeta = _proj_decay(hn, wt, o_z, alog_pad, dtb_pad)
    glu = _proj_glu(hn, w_all, o_z // TN, conf_w)
    gates = _proj(_mm_sigmoid_kernel, hn, w_all, (o_z + 2 * conf_w) // TN, 2 * d, BF16, "proj_gates")

    wc = w_conv_qkv[0]
    meta_blk = (tp + ts) // n_meta
    qkv_meta = _prep_long(qkvz, wc, n_meta, n_meta, meta_blk, lambda i: 0, True, "prep_meta")
    rows_p = 256
    bps = seq // rows_p
    meta_halo = (tp + ts + n_meta) // 8 - 1
    qkv_p = _prep_long(qkvz, wc, tp, rows_p, 0,
                       lambda i: jnp.where(i % bps == 0, meta_halo, i * (rows_p // 8) - 1), False, "prep_prompt")
    raw_s = qkvz[tp:tp + ts, :qkv_w].reshape(steps, nsmp, qkv_w)
    st_qkv_tm = state_conv_qkv[0].transpose(1, 0, 2)
    qkv_s_tm = _prep_short(raw_s, st_qkv_tm, wc)

    zero_state = jnp.zeros((1, N_HEADS, HEAD_DIM, HEAD_DIM), F32)
    oz_meta, s_meta = _delta(qkv_meta, gbeta, qkvz, 3, zero_state, w_onorm[0], 1, 1, n_meta,
                             lambda n, ci: 0, lambda n, ci: meta_blk, lambda n: 0, "delta_meta")
    n_chunks = seq // CHUNK
    prompt_blk = lambda n, ci: n * n_chunks + ci
    oz_p, s_p = _delta(qkv_p, gbeta, qkvz, 3, s_meta, w_onorm[0], bsz, n_chunks, CHUNK,
                       prompt_blk, prompt_blk, lambda n: 0, "delta_prompt")

    cpad = 8
    to_bm = lambda a: jnp.pad(a.transpose(1, 0, 2), ((0, 0), (0, cpad - steps), (0, 0))).reshape(nsmp * cpad, a.shape[-1])
    qkv_s = to_bm(qkv_s_tm)
    gb_s = to_bm(gbeta[tp:tp + ts].reshape(steps, nsmp, LANES))
    z_s = to_bm(qkvz[tp:tp + ts, qkv_w:].reshape(steps, nsmp, DN_WIDTH))
    oz_s_bm, s_s = _delta(qkv_s, gb_s, z_s, 0, state_delta[0], w_onorm[0], nsmp, 1, cpad,
                          lambda n, ci: n, lambda n, ci: n, lambda n: n, "delta_sample")
    oz_s = oz_s_bm.reshape(nsmp, cpad, DN_WIDTH)[:, :steps].transpose(1, 0, 2).reshape(ts, DN_WIDTH)
    tail_pad = lambda w: jnp.zeros((tail_rows - ts - n_meta, w), BF16)
    oz_t = jnp.concatenate([oz_s, oz_meta, tail_pad(DN_WIDTH)], axis=0)

    zeros32 = jnp.zeros((32, conf_w), F32)
    cz_meta = _conf_long(glu, zeros32, w_dw[0], b_dw[0], ln_w[0], ln_b[0], n_meta, n_meta, meta_blk, 1,
                         lambda i: 0, "conf_meta")
    init_p = jnp.concatenate([jnp.zeros((32 - n_meta, conf_w), F32), glu[tp + ts:]], axis=0)
    rows_c = 128
    cz_p = _conf_long(glu, init_p, w_dw[0], b_dw[0], ln_w[0], ln_b[0], tp, rows_c, 0, seq // rows_c,
                      lambda i: jnp.maximum(i * (rows_c // 32) - 1, 0), "conf_prompt")
    glu_s_tm = glu[tp:tp + ts].reshape(steps, nsmp, conf_w)
    hist_s_tm = state_conv_b[0].transpose(1, 0, 2)
    cz_s = _conf_short(glu_s_tm, hist_s_tm, w_dw[0], b_dw[0], ln_w[0], ln_b[0]).reshape(ts, conf_w)
    cz_t = jnp.concatenate([cz_s, cz_meta, tail_pad(conf_w)], axis=0)

    merged = _merge(oz_p, oz_t, cz_p, cz_t, gates, w_proj_a[0].astype(BF16), w_proj_b[0].astype(BF16), b_proj_b[0])
    h1 = _out_proj(merged, w_out[0].astype(BF16), x_p, x_tail)

    w_r = jnp.pad(jnp.concatenate([w_rg[0], w_re[0]], axis=1), ((0, 0), (0, LANES - N_GROUPS - N_EXPERTS)))
    b_r = jnp.pad(jnp.concatenate([b_rg[0], b_re[0]]), (0, LANES - N_GROUPS - N_EXPERTS)).reshape(1, LANES)
    xn2, ids, wts = _router(h1, norm2[0], w_r, b_r)
    max_visits = N_EXPERTS + (2 * t_all) // GROUP_ROWS
    row_slot, visit_expert, n_visits, visit_rows, visit_start = _dispatch_plan(ids, max_visits)
    eo = _experts(visit_expert, n_visits, visit_rows, visit_start, row_slot, xn2,
                  w_e_gate[0], w_e_up[0], w_e_down[0], max_visits)
    y_p = _final(h1, eo, wts, final_norm, tp, 0)
    y_s = _final(h1, eo, wts, final_norm, ts, tp // FINAL_ROWS)

    y_prompt = y_p.reshape(bsz, seq, d)
    y_sample = y_s.reshape(steps, nsmp, d).transpose(1, 0, 2)
    new_cq_p = jnp.stack([qkvz[(b + 1) * seq - (SHORT_CONV - 1):(b + 1) * seq, :qkv_w] for b in range(bsz)])
    new_cb_p = jnp.stack([glu[(b + 1) * seq - hist:(b + 1) * seq] for b in range(bsz)])
    cq_s = jnp.concatenate([st_qkv_tm, raw_s], axis=0)[-(SHORT_CONV - 1):].transpose(1, 0, 2)
    cb_s = jnp.concatenate([hist_s_tm, glu_s_tm], axis=0)[-hist:].transpose(1, 0, 2)
    return (y_prompt, y_sample, s_p[None], new_cq_p[None], new_cb_p[None], s_s[None], cq_s[None], cb_s[None])
```

```python
import functools

import jax
import jax.numpy as jnp
from jax import lax
from jax.experimental import pallas as pl
from jax.experimental.pallas import tpu as pltpu

F32 = jnp.float32
BF16 = jnp.bfloat16
EPS = 1e-6

N_HEADS = 16
HEAD_DIM = 128
DN_WIDTH = N_HEADS * HEAD_DIM
SHORT_CONV = 4
CONF_KERNEL = 31
N_GROUPS = 8
EXPERTS_PER_GROUP = 8
N_EXPERTS = N_GROUPS * EXPERTS_PER_GROUP
CHUNK = 64
LANES = 128
HIGHEST = lax.Precision.HIGHEST

TM = 1024
TN = 512
GROUP_ROWS = 320
EXPERT_F_TILE = 256
DMA_UNROLL = 8
ROW_DMA_PRIORITY = 1
CONF_LANE_CHUNK = 512
FINAL_ROWS = 256
VMEM_LIMIT = 56 * 1024 * 1024


def _cp(*sem):
    return pltpu.CompilerParams(dimension_semantics=sem, vmem_limit_bytes=VMEM_LIMIT)


def _sigmoid(x):
    return 1.0 / (1.0 + jnp.exp(-x))


def _silu(x):
    return x * _sigmoid(x)


def _softplus(x):
    return jnp.maximum(x, 0.0) + jnp.log1p(jnp.exp(-jnp.abs(x)))


def _rmsnorm_kernel(xp_ref, xt_ref, w_ref, o_ref, *, prompt_blocks):
    def norm(x):
        ms = jnp.mean(x * x, axis=-1, keepdims=True)
        return (x * lax.rsqrt(ms + EPS) * w_ref[...]).astype(o_ref.dtype)

    @pl.when(pl.program_id(0) < prompt_blocks)
    def _():
        o_ref[...] = norm(xp_ref[...])

    @pl.when(pl.program_id(0) >= prompt_blocks)
    def _():
        o_ref[...] = norm(xt_ref[...])


def _rmsnorm_bf16(x_prompt, x_tail, w, t_all, rows=256):
    tp, d = x_prompt.shape
    nbp = tp // rows
    return pl.pallas_call(
        functools.partial(_rmsnorm_kernel, prompt_blocks=nbp),
        grid=(pl.cdiv(t_all, rows),),
        in_specs=[
            pl.BlockSpec((rows, d), lambda i: (jnp.minimum(i, nbp - 1), 0)),
            pl.BlockSpec((rows, d), lambda i: (jnp.maximum(i - nbp, 0), 0)),
            pl.BlockSpec((1, d), lambda i: (0, 0)),
        ],
        out_specs=pl.BlockSpec((rows, d), lambda i: (i, 0)),
        out_shape=jax.ShapeDtypeStruct((t_all, d), BF16),
        compiler_params=_cp("parallel"),
        name="rmsnorm1",
    )(x_prompt, x_tail, w.reshape(1, d))


def _mm_plain_kernel(x_ref, w_ref, o_ref):
    o_ref[...] = jnp.dot(x_ref[...], w_ref[...], preferred_element_type=F32).astype(o_ref.dtype)


def _mm_sigmoid_kernel(x_ref, w_ref, o_ref):
    o_ref[...] = _sigmoid(jnp.dot(x_ref[...], w_ref[...], preferred_element_type=F32)).astype(o_ref.dtype)


def _mm_glu_kernel(x_ref, wu_ref, wg_ref, o_ref):
    x = x_ref[...]
    u = jnp.dot(x, wu_ref[...], preferred_element_type=F32)
    g = jnp.dot(x, wg_ref[...], preferred_element_type=F32)
    o_ref[...] = u * _sigmoid(g)


def _mm_decay_kernel(x_ref, w_ref, alog_ref, dtb_ref, o_ref):
    acc = lax.dot_general(x_ref[...], w_ref[...].astype(BF16), (((1,), (1,)), ((), ())), preferred_element_type=F32)
    lane = lax.broadcasted_iota(jnp.int32, acc.shape, 1)
    g = -jnp.exp(alog_ref[...]) * _softplus(acc + dtb_ref[...])
    beta = _sigmoid(acc)
    o_ref[...] = jnp.where(lane < N_HEADS, g, jnp.where(lane < 2 * N_HEADS, beta, 0.0))


def _realign_kernel(a_ref, b_ref, o_ref, *, aligned_blocks, shift):
    j = pl.program_id(0)

    @pl.when(j < aligned_blocks)
    def _():
        o_ref[...] = a_ref[...].T.astype(o_ref.dtype)

    @pl.when(j >= aligned_blocks)
    def _():
        rows = jnp.concatenate([a_ref[shift:, :], b_ref[...]], axis=0)
        o_ref[...] = rows.T.astype(o_ref.dtype)


def _realign_w_in(wt, aligned_cols, shift, out_cols):
    n, k = wt.shape
    return pl.pallas_call(
        functools.partial(_realign_kernel, aligned_blocks=aligned_cols // TN, shift=shift),
        grid=(out_cols // TN,),
        in_specs=[
            pl.BlockSpec((TN, k), lambda j: (j, 0)),
            pl.BlockSpec((shift, k), lambda j: ((j + 1) * (TN // shift), 0)),
        ],
        out_specs=pl.BlockSpec((k, TN), lambda j: (0, j)),
        out_shape=jax.ShapeDtypeStruct((k, out_cols), BF16),
        compiler_params=_cp("parallel"),
        name="realign_w_in",
    )(wt, wt)


def _proj(kernel_fn, hn, w, col_block0, n_out, out_dtype, name, tn=TN):
    t, k = hn.shape
    return pl.pallas_call(
        kernel_fn,
        grid=(pl.cdiv(t, TM), n_out // tn),
        in_specs=[pl.BlockSpec((TM, k), lambda i, j: (i, 0)), pl.BlockSpec((k, tn), lambda i, j: (0, j + col_block0))],
        out_specs=pl.BlockSpec((TM, tn), lambda i, j: (i, j)),
        out_shape=jax.ShapeDtypeStruct((t, n_out), out_dtype),
        compiler_params=_cp("parallel", "arbitrary"),
        name=name,
    )(hn, w)


def _proj_glu(hn, w, col_block0, half):
    t, k = hn.shape
    nb = half // TN
    return pl.pallas_call(
        _mm_glu_kernel,
        grid=(pl.cdiv(t, TM), nb),
        in_specs=[
            pl.BlockSpec((TM, k), lambda i, j: (i, 0)),
            pl.BlockSpec((k, TN), lambda i, j: (0, j + col_block0)),
            pl.BlockSpec((k, TN), lambda i, j: (0, j + col_block0 + nb)),
        ],
        out_specs=pl.BlockSpec((TM, TN), lambda i, j: (i, j)),
        out_shape=jax.ShapeDtypeStruct((t, half), F32),
        compiler_params=_cp("parallel", "arbitrary"),
        name="proj_glu",
    )(hn, w, w)


def _proj_decay(hn, wt, row0, alog_pad, dtb_pad):
    t, k = hn.shape
    return pl.pallas_call(
        _mm_decay_kernel,
        grid=(pl.cdiv(t, TM),),
        in_specs=[
            pl.BlockSpec((TM, k), lambda i: (i, 0)),
            pl.BlockSpec((LANES, k), lambda i: (row0 // LANES, 0)),
            pl.BlockSpec((1, LANES), lambda i: (0, 0)),
            pl.BlockSpec((1, LANES), lambda i: (0, 0)),
        ],
        out_specs=pl.BlockSpec((TM, LANES), lambda i: (i, 0)),
        out_shape=jax.ShapeDtypeStruct((t, LANES), F32),
        compiler_params=_cp("parallel"),
        name="proj_decay",
    )(hn, wt, alog_pad, dtb_pad)


def _head_normalize(y, sec):
    outs = []
    for h in range(N_HEADS):
        yh = y[:, h * HEAD_DIM:(h + 1) * HEAD_DIM]
        inv = lax.rsqrt(jnp.sum(yh * yh, axis=-1, keepdims=True) + EPS)
        scale = jnp.where(sec == 0, inv * (HEAD_DIM ** -0.5), jnp.where(sec == 1, inv, 1.0))
        outs.append(yh * scale)
    return jnp.concatenate(outs, axis=-1)


def _prep_long_kernel(cur_ref, halo_ref, w_ref, o_ref, ext_ref, *, rows, zero_halo):
    sec = pl.program_id(1)
    halo = halo_ref[...]
    ext_ref[0:8, :] = jnp.zeros_like(halo) if zero_halo else halo
    ext_ref[8:, :] = cur_ref[...]
    acc = w_ref[SHORT_CONV - 1:SHORT_CONV, :] * cur_ref[...]
    for s in range(1, SHORT_CONV):
        acc = acc + w_ref[SHORT_CONV - 1 - s:SHORT_CONV - s, :] * ext_ref[pl.ds(8 - s, rows), :]
    o_ref[...] = _head_normalize(_silu(acc), sec)


def _prep_long(qkvz, w_conv, n_rows, rows, row_block0, halo_index, zero_halo, name):
    nblk = n_rows // rows
    kern = functools.partial(_prep_long_kernel, rows=rows, zero_halo=zero_halo)
    return pl.pallas_call(
        kern,
        grid=(nblk, 3),
        in_specs=[
            pl.BlockSpec((rows, DN_WIDTH), lambda i, s: (i + row_block0, s)),
            pl.BlockSpec((8, DN_WIDTH), lambda i, s: (halo_index(i), s)),
            pl.BlockSpec((SHORT_CONV, DN_WIDTH), lambda i, s: (0, s)),
        ],
        out_specs=pl.BlockSpec((rows, DN_WIDTH), lambda i, s: (i, s)),
        out_shape=jax.ShapeDtypeStruct((n_rows, 3 * DN_WIDTH), F32),
        scratch_shapes=[pltpu.VMEM((rows + 8, DN_WIDTH), F32)],
        compiler_params=_cp("parallel", "arbitrary"),
        name=name,
    )(qkvz, qkvz, w_conv)


def _prep_short_kernel(x_ref, st_ref, w_ref, o_ref, *, steps):
    sec = pl.program_id(0)
    buf = [st_ref[i] for i in range(SHORT_CONV - 1)] + [x_ref[t] for t in range(steps)]
    for t in range(steps):
        acc = w_ref[0:1, :] * buf[t]
        for i in range(1, SHORT_CONV):
            acc = acc + w_ref[i:i + 1, :] * buf[t + i]
        o_ref[t] = _head_normalize(_silu(acc), sec)


def _prep_short(x_tm, state_tm, w_conv):
    steps, n, _ = x_tm.shape
    kern = functools.partial(_prep_short_kernel, steps=steps)
    return pl.pallas_call(
        kern,
        grid=(3,),
        in_specs=[
            pl.BlockSpec((steps, n, DN_WIDTH), lambda s: (0, 0, s)),
            pl.BlockSpec((SHORT_CONV - 1, n, DN_WIDTH), lambda s: (0, 0, s)),
            pl.BlockSpec((SHORT_CONV, DN_WIDTH), lambda s: (0, s)),
        ],
        out_specs=pl.BlockSpec((steps, n, DN_WIDTH), lambda s: (0, 0, s)),
        out_shape=jax.ShapeDtypeStruct((steps, n, 3 * DN_WIDTH), F32),
        compiler_params=_cp("arbitrary"),
        name="prep_sample",
    )(x_tm, state_tm, w_conv)


def _hdot(a, b):
    return jnp.dot(a, b, preferred_element_type=F32, precision=HIGHEST)


def _split_bf16(a):
    hi = a.astype(BF16)
    lo = (a - hi.astype(F32)).astype(BF16)
    return hi, lo


def _bmm(a, b):
    return jnp.einsum("hik,hkj->hij", a.astype(BF16), b.astype(BF16), preferred_element_type=F32)


def _bmm3(a, b):
    ah, al = _split_bf16(a)
    bh, bl = _split_bf16(b)
    f = lambda x, y: jnp.einsum("hik,hkj->hij", x, y, preferred_element_type=F32)
    return f(ah, bh) + (f(ah, bl) + f(al, bh))


def _unit_lower_inverse(low, c):
    base = min(c, 16)
    row = lax.broadcasted_iota(jnp.int32, (c, c), 0)
    col = lax.broadcasted_iota(jnp.int32, (c, c), 1)
    eye = (row == col).astype(F32)[None]
    diag = jnp.where(((row // base) == (col // base))[None], low, 0.0) if c > base else low
    inv = eye - diag
    power = diag
    k = 2
    while k < base:
        power = _bmm3(power, power)
        inv = inv + _bmm3(inv, power)
        k *= 2
    blk = base
    while blk < c:
        sel = ((row // (2 * blk)) == (col // (2 * blk))) & (((row // blk) % 2) == 1) & (((col // blk) % 2) == 0)
        off = jnp.where(sel[None], low, 0.0)
        inv = inv - _bmm3(inv, _bmm3(off, inv))
        blk *= 2
    return inv


def _delta_kernel(q_ref, k_ref, v_ref, gb_ref, z_ref, s0_ref, wn_ref, o_ref, sout_ref, s_scr, *, c, heads_per_group):
    ci = pl.program_id(1)

    @pl.when(ci == 0)
    def _():
        s_scr[...] = s0_ref[0]

    row = lax.broadcasted_iota(jnp.int32, (c, c), 0)
    col = lax.broadcasted_iota(jnp.int32, (c, c), 1)
    causal = (row >= col)[None]
    strict = (row > col)[None]
    gb = gb_ref[...]
    gcum = _hdot((row >= col).astype(F32), gb)
    gcum_t = gcum.T
    wn = wn_ref[...]
    d = HEAD_DIM
    for h0 in range(0, N_HEADS, heads_per_group):
        heads = range(h0, h0 + heads_per_group)
        hs = slice(h0, h0 + heads_per_group)
        heads_of = lambda ref: jnp.stack([ref[:, h * d:(h + 1) * d] for h in heads])
        q = heads_of(q_ref)
        k = heads_of(k_ref)
        v = heads_of(v_ref)
        g_col = jnp.stack([gcum[:, h:h + 1] for h in heads])
        g_row = jnp.stack([gcum_t[h:h + 1, :] for h in heads])
        beta = jnp.stack([gb[:, N_HEADS + h:N_HEADS + h + 1] for h in heads])
        g_last = g_col[:, c - 1:c, :]
        decay = jnp.where(causal, jnp.exp(g_col - g_row), 0.0)
        exp_g = jnp.exp(g_col)
        kb = k * beta
        kk = jnp.einsum("hid,hjd->hij", jnp.concatenate([kb, q], axis=1).astype(BF16), k.astype(BF16),
                        preferred_element_type=F32)
        lower = jnp.where(strict, kk[:, :c] * decay, 0.0)
        attn = kk[:, c:] * decay
        tinv = _unit_lower_inverse(lower, c)
        rhs = jnp.concatenate([v * beta, kb * exp_g], axis=-1)
        sol = _bmm3(tinv, rhs)
        s = s_scr[hs]
        both = _bmm(jnp.concatenate([sol[:, :, d:], q * exp_g], axis=1), s)
        u = sol[:, :, :d] - both[:, :c]
        o = both[:, c:] + _bmm(attn, u)
        kd = k * jnp.exp(g_last - g_col)
        upd = lax.dot_general(kd.astype(BF16), u.astype(BF16), (((1,), (1,)), ((0,), (0,))),
                              preferred_element_type=F32)
        s_scr[hs] = s * jnp.exp(g_last) + upd
        on = o * lax.rsqrt(jnp.mean(o * o, axis=-1, keepdims=True) + EPS) * wn
        out = (on * _silu(heads_of(z_ref))).astype(o_ref.dtype)
        for i, h in enumerate(heads):
            o_ref[:, h * d:(h + 1) * d] = out[i]

    @pl.when(ci == pl.num_programs(1) - 1)
    def _():
        sout_ref[0] = s_scr[...]


def _delta(qkv, gb, zsrc, z_col_block, s0, w_onorm, n_seq, n_chunks, c, row_block, aux_block, s0_index, name,
           heads_per_group=N_HEADS):
    kern = functools.partial(_delta_kernel, c=c, heads_per_group=heads_per_group)
    rows_out = n_seq * n_chunks * c
    return pl.pallas_call(
        kern,
        grid=(n_seq, n_chunks),
        in_specs=[
            pl.BlockSpec((c, DN_WIDTH), lambda n, ci: (row_block(n, ci), 0)),
            pl.BlockSpec((c, DN_WIDTH), lambda n, ci: (row_block(n, ci), 1)),
            pl.BlockSpec((c, DN_WIDTH), lambda n, ci: (row_block(n, ci), 2)),
            pl.BlockSpec((c, LANES), lambda n, ci: (aux_block(n, ci), 0)),
            pl.BlockSpec((c, DN_WIDTH), lambda n, ci: (aux_block(n, ci), z_col_block)),
            pl.BlockSpec((1, N_HEADS, HEAD_DIM, HEAD_DIM), lambda n, ci: (s0_index(n), 0, 0, 0)),
            pl.BlockSpec((1, HEAD_DIM), lambda n, ci: (0, 0)),
        ],
        out_specs=[
            pl.BlockSpec((c, DN_WIDTH), lambda n, ci: (n * n_chunks + ci, 0)),
            pl.BlockSpec((1, N_HEADS, HEAD_DIM, HEAD_DIM), lambda n, ci: (n, 0, 0, 0)),
        ],
        out_shape=[
            jax.ShapeDtypeStruct((rows_out, DN_WIDTH), BF16),
            jax.ShapeDtypeStruct((n_seq, N_HEADS, HEAD_DIM, HEAD_DIM), F32),
        ],
        scratch_shapes=[pltpu.VMEM((N_HEADS, HEAD_DIM, HEAD_DIM), F32)],
        compiler_params=_cp("parallel", "arbitrary"),
        name=name,
    )(qkv, qkv, qkv, gb, zsrc, s0, w_onorm.reshape(1, HEAD_DIM))


def _layernorm_silu(c, lnw, lnb):
    mu = jnp.mean(c, axis=-1, keepdims=True)
    xc = c - mu
    y = xc * lax.rsqrt(jnp.mean(xc * xc, axis=-1, keepdims=True) + EPS)
    return _silu(y * lnw + lnb)


def _conf_long_kernel(cur_ref, prev_ref, init_ref, w_ref, b_ref, lnw_ref, lnb_ref, o_ref, ext_ref, c_ref, *, rows, blocks_per_seq):
    hist = CONF_KERNEL - 1
    first = (pl.program_id(0) % blocks_per_seq) == 0
    ext_ref[0:32, :] = jnp.where(first, init_ref[...], prev_ref[...])
    ext_ref[32:, :] = cur_ref[...]
    cw = cur_ref.shape[1]
    lc = min(CONF_LANE_CHUNK, cw)
    rowid = lax.broadcasted_iota(jnp.int32, (8, lc), 0)
    for c0 in range(0, cw, lc):
        cs = slice(c0, c0 + lc)

        def qtile(n, r):
            acc = None
            for a in range(5):
                i = 8 * a + r - 2
                if 0 <= i <= hist:
                    term = w_ref[8 * i:8 * i + 8, cs] * ext_ref[8 * (n + a):8 * (n + a) + 8, cs]
                    acc = term if acc is None else acc + term
            return acc

        held = [None] + [qtile(0, r) for r in range(1, 8)]
        for m in range(rows // 8):
            out = qtile(m, 0)
            for r in range(1, 8):
                nxt = qtile(m + 1, r)
                out = out + pltpu.roll(jnp.where(rowid >= r, held[r], nxt), 8 - r, axis=0)
                held[r] = nxt
            c_ref[8 * m:8 * m + 8, cs] = out
    o_ref[...] = _layernorm_silu(c_ref[...] + b_ref[...], lnw_ref[...], lnb_ref[...]).astype(o_ref.dtype)


def _conf_long(glu, init_hist, w_dw, b_dw, ln_w, ln_b, n_rows, rows, row_block0, blocks_per_seq, prev_index, name):
    cw = glu.shape[1]
    nblk = n_rows // rows
    kern = functools.partial(_conf_long_kernel, rows=rows, blocks_per_seq=blocks_per_seq)
    vec = lambda a: a.reshape(1, cw)
    return pl.pallas_call(
        kern,
        grid=(nblk,),
        in_specs=[
            pl.BlockSpec((rows, cw), lambda i: (i + row_block0, 0)),
            pl.BlockSpec((32, cw), lambda i: (prev_index(i), 0)),
            pl.BlockSpec((32, cw), lambda i: (0, 0)),
            pl.BlockSpec((8 * CONF_KERNEL, cw), lambda i: (0, 0)),
            pl.BlockSpec((1, cw), lambda i: (0, 0)),
            pl.BlockSpec((1, cw), lambda i: (0, 0)),
            pl.BlockSpec((1, cw), lambda i: (0, 0)),
        ],
        out_specs=pl.BlockSpec((rows, cw), lambda i: (i, 0)),
        out_shape=jax.ShapeDtypeStruct((n_rows, cw), BF16),
        scratch_shapes=[pltpu.VMEM((rows + 32, cw), F32), pltpu.VMEM((rows, cw), F32)],
        compiler_params=_cp("parallel"),
        name=name,
    )(glu, glu, init_hist, jnp.repeat(w_dw, 8, axis=0), vec(b_dw), vec(ln_w), vec(ln_b))


def _conf_short_kernel(x_ref, hist_ref, w_ref, b_ref, lnw_ref, lnb_ref, o_ref, *, steps):
    nh = CONF_KERNEL - 1
    for t in range(steps):
        acc = jnp.zeros(x_ref.shape[1:], F32)
        for i in range(CONF_KERNEL):
            j = t + i
            src = hist_ref[j] if j < nh else x_ref[j - nh]
            acc = acc + w_ref[i:i + 1, :] * src
        o_ref[t] = _layernorm_silu(acc + b_ref[...], lnw_ref[...], lnb_ref[...]).astype(o_ref.dtype)


def _conf_short(x_tm, hist_tm, w_dw, b_dw, ln_w, ln_b, nb=32):
    steps, n, cw = x_tm.shape
    kern = functools.partial(_conf_short_kernel, steps=steps)
    vec = lambda a: a.reshape(1, cw)
    return pl.pallas_call(
        kern,
        grid=(n // nb,),
        in_specs=[
            pl.BlockSpec((steps, nb, cw), lambda i: (0, i, 0)),
            pl.BlockSpec((CONF_KERNEL - 1, nb, cw), lambda i: (0, i, 0)),
            pl.BlockSpec((CONF_KERNEL, cw), lambda i: (0, 0)),
            pl.BlockSpec((1, cw), lambda i: (0, 0)),
            pl.BlockSpec((1, cw), lambda i: (0, 0)),
            pl.BlockSpec((1, cw), lambda i: (0, 0)),
        ],
        out_specs=pl.BlockSpec((steps, nb, cw), lambda i: (0, i, 0)),
        out_shape=jax.ShapeDtypeStruct((steps, n, cw), BF16),
        compiler_params=_cp("parallel"),
        name="conf_sample",
    )(x_tm, hist_tm, w_dw, vec(b_dw), vec(ln_w), vec(ln_b))


def _merge_kernel(ozp_ref, ozt_ref, czp_ref, czt_ref, ga_ref, gb_ref, wa_ref, wb_ref, bias_ref, o_ref, *, prompt_blocks):
    def body(oz_ref, cz_ref):
        ya = jnp.dot(oz_ref[...], wa_ref[...], preferred_element_type=F32)
        yb = jnp.dot(cz_ref[...], wb_ref[...], preferred_element_type=F32) + bias_ref[...]
        o_ref[...] = (ga_ref[...].astype(F32) * ya + gb_ref[...].astype(F32) * yb).astype(o_ref.dtype)

    @pl.when(pl.program_id(0) < prompt_blocks)
    def _():
        body(ozp_ref, czp_ref)

    @pl.when(pl.program_id(0) >= prompt_blocks)
    def _():
        body(ozt_ref, czt_ref)


def _merge(oz_p, oz_t, cz_p, cz_t, gates, wa, wb, bias):
    t = gates.shape[0]
    ka = oz_p.shape[1]
    kb = cz_p.shape[1]
    d = wa.shape[1]
    nb = d // TN
    nbp = oz_p.shape[0] // TM
    first = lambda i, j: (jnp.minimum(i, nbp - 1), 0)
    rest = lambda i, j: (jnp.maximum(i - nbp, 0), 0)
    return pl.pallas_call(
        functools.partial(_merge_kernel, prompt_blocks=nbp),
        grid=(pl.cdiv(t, TM), nb),
        in_specs=[
            pl.BlockSpec((TM, ka), first),
            pl.BlockSpec((TM, ka), rest),
            pl.BlockSpec((TM, kb), first),
            pl.BlockSpec((TM, kb), rest),
            pl.BlockSpec((TM, TN), lambda i, j: (i, j)),
            pl.BlockSpec((TM, TN), lambda i, j: (i, j + nb)),
            pl.BlockSpec((ka, TN), lambda i, j: (0, j)),
            pl.BlockSpec((kb, TN), lambda i, j: (0, j)),
            pl.BlockSpec((1, TN), lambda i, j: (0, j)),
        ],
        out_specs=pl.BlockSpec((TM, TN), lambda i, j: (i, j)),
        out_shape=jax.ShapeDtypeStruct((t, d), BF16),
        compiler_params=_cp("parallel", "arbitrary"),
        name="merge",
    )(oz_p, oz_t, cz_p, cz_t, gates, gates, wa, wb, bias.reshape(1, d))


def _out_kernel(m_ref, w_ref, xp_ref, xt_ref, o_ref, *, prompt_blocks):
    acc = jnp.dot(m_ref[...], w_ref[...], preferred_element_type=F32)

    @pl.when(pl.program_id(0) < prompt_blocks)
    def _():
        o_ref[...] = xp_ref[...] + acc

    @pl.when(pl.program_id(0) >= prompt_blocks)
    def _():
        o_ref[...] = xt_ref[...] + acc


def _out_proj(merged, w_out, x_prompt, x_tail):
    t, k = merged.shape
    d = w_out.shape[1]
    nbp = x_prompt.shape[0] // TM
    return pl.pallas_call(
        functools.partial(_out_kernel, prompt_blocks=nbp),
        grid=(pl.cdiv(t, TM), d // TN),
        in_specs=[
            pl.BlockSpec((TM, k), lambda i, j: (i, 0)),
            pl.BlockSpec((k, TN), lambda i, j: (0, j)),
            pl.BlockSpec((TM, TN), lambda i, j: (jnp.minimum(i, nbp - 1), j)),
            pl.BlockSpec((TM, TN), lambda i, j: (jnp.maximum(i - nbp, 0), jnp.where(i >= nbp, j, 0))),
        ],
        out_specs=pl.BlockSpec((TM, TN), lambda i, j: (i, j)),
        out_shape=jax.ShapeDtypeStruct((t, d), F32),
        compiler_params=_cp("parallel", "arbitrary"),
        name="out_proj",
    )(merged, w_out, x_prompt, x_tail)


def _pack_bf16_pairs(x):
    half = x.shape[1] // 2
    bits = lax.bitcast_convert_type(x.astype(BF16).astype(F32), jnp.uint32)
    return (bits[:, :half] >> 16) | (bits[:, half:] & jnp.uint32(0xFFFF0000))


def _unpack_bf16_pairs(w):
    lo = lax.bitcast_convert_type(w << 16, F32).astype(BF16)
    hi = lax.bitcast_convert_type(w & jnp.uint32(0xFFFF0000), F32).astype(BF16)
    return lo, hi


def _router_kernel(h_ref, nw_ref, wr_ref, br_ref, xn_ref, ids_ref, wts_ref):
    x = h_ref[...]
    xn = x * lax.rsqrt(jnp.mean(x * x, axis=-1, keepdims=True) + EPS) * nw_ref[...]
    xn_ref[...] = _pack_bf16_pairs(xn)
    xh, xl = _split_bf16(xn)
    wh, wl = _split_bf16(wr_ref[...])
    dot = lambda a, b: jnp.dot(a, b, preferred_element_type=F32)
    logits = dot(xh, wh) + (dot(xh, wl) + dot(xl, wh)) + br_ref[...]
    lane = lax.broadcasted_iota(jnp.int32, logits.shape, 1)
    neg = -jnp.inf
    big = jnp.int32(1 << 20)
    gl = jnp.where(lane < N_GROUPS, logits, neg)
    gmax = jnp.max(gl, axis=-1, keepdims=True)
    gsum = jnp.sum(jnp.exp(gl - gmax), axis=-1, keepdims=True)
    pg_top = 1.0 / gsum
    gidx = jnp.min(jnp.where(gl == gmax, lane, big), axis=-1, keepdims=True)
    lo = N_GROUPS + gidx * EXPERTS_PER_GROUP
    in_grp = (lane >= lo) & (lane < lo + EXPERTS_PER_GROUP)
    el = jnp.where(in_grp, logits, neg)
    emax = jnp.max(el, axis=-1, keepdims=True)
    ex = jnp.exp(el - emax)
    esum = jnp.sum(ex, axis=-1, keepdims=True)
    pe = ex / esum
    pe = jnp.where(in_grp, pe, -1.0)
    p1 = jnp.max(pe, axis=-1, keepdims=True)
    i1 = jnp.min(jnp.where(pe == p1, lane, big), axis=-1, keepdims=True)
    pe2 = jnp.where(lane == i1, -1.0, pe)
    p2 = jnp.max(pe2, axis=-1, keepdims=True)
    i2 = jnp.min(jnp.where(pe2 == p2, lane, big), axis=-1, keepdims=True)
    denom = p1 + p2
    w1 = pg_top * p1 / denom
    w2 = pg_top * p2 / denom
    ids_ref[...] = jnp.where(lane == 0, i1 - N_GROUPS, jnp.where(lane == 1, i2 - N_GROUPS, 0))
    wts_ref[...] = jnp.where(lane == 0, w1, jnp.where(lane == 1, w2, 0.0))


def _router(h1, norm_w, w_r, b_r, rows=256):
    t, d = h1.shape
    return pl.pallas_call(
        _router_kernel,
        grid=(pl.cdiv(t, rows),),
        in_specs=[
            pl.BlockSpec((rows, d), lambda i: (i, 0)),
            pl.BlockSpec((1, d), lambda i: (0, 0)),
            pl.BlockSpec((d, LANES), lambda i: (0, 0)),
            pl.BlockSpec((1, LANES), lambda i: (0, 0)),
        ],
        out_specs=[
            pl.BlockSpec((rows, d // 2), lambda i: (i, 0)),
            pl.BlockSpec((rows, LANES), lambda i: (i, 0)),
            pl.BlockSpec((rows, LANES), lambda i: (i, 0)),
        ],
        out_shape=[
            jax.ShapeDtypeStruct((t, d // 2), jnp.uint32),
            jax.ShapeDtypeStruct((t, LANES), jnp.int32),
            jax.ShapeDtypeStruct((t, LANES), F32),
        ],
        compiler_params=_cp("parallel"),
        name="router",
    )(h1, norm_w.reshape(1, d), w_r, b_r)


def _experts_kernel(ve_ref, nv_ref, vr_ref, vs_ref, slot_ref, xn_hbm, wg_ref, wu_ref, wd_ref, eo_hbm,
                    xbuf, xlo, xhi, hmid, wdb, acc_ref, gsem, ssem):
    v = pl.program_id(0)
    f = pl.program_id(1)
    nf = pl.num_programs(1)
    nv = nv_ref[0]
    half = xbuf.shape[1]

    def gather_copy(visit, r):
        tok = slot_ref[vs_ref[visit] + r] >> 1
        return pltpu.make_async_copy(xn_hbm.at[pl.ds(tok, 1), :], xbuf.at[pl.ds(r, 1), :], gsem)

    def scatter_copy(visit, r):
        slot = slot_ref[vs_ref[visit] + r]
        return pltpu.make_async_copy(acc_ref.at[pl.ds(r, 1), :], eo_hbm.at[slot & 1, pl.ds(slot >> 1, 1), :], ssem)

    gather_rows_wait = lambda: pltpu.make_async_copy(
        xn_hbm.at[pl.ds(0, DMA_UNROLL), :], xbuf.at[pl.ds(0, DMA_UNROLL), :], gsem).wait()
    scatter_rows_wait = lambda: pltpu.make_async_copy(
        acc_ref.at[pl.ds(0, DMA_UNROLL), :], eo_hbm.at[0, pl.ds(0, DMA_UNROLL), :], ssem).wait()

    def loop(lo, hi, fn):
        def body(i, carry):
            fn(i)
            return carry
        lax.fori_loop(lo, hi, body, 0)

    def unrolled(fn):
        def run(i):
            for j in range(DMA_UNROLL):
                fn(i * DMA_UNROLL + j)
        return run

    nparts = hmid.shape[0]
    unit = DMA_UNROLL
    gather_groups = -(-GROUP_ROWS // (nparts * unit))
    scatter_rows = -(-GROUP_ROWS // ((nparts - 1) * unit)) * unit

    def start_gather(visit, part):
        groups = (vr_ref[visit] + unit - 1) // unit
        lo = part * gather_groups
        loop(lo, jnp.minimum(lo + gather_groups, groups),
             unrolled(lambda r: gather_copy(visit, r).start(priority=ROW_DMA_PRIORITY)))

    def wait_gather(visit):
        loop(0, (vr_ref[visit] + unit - 1) // unit, lambda i: gather_rows_wait())

    def start_scatter(visit, part):
        lo = part * scatter_rows
        hi = jnp.minimum(lo + scatter_rows, vr_ref[visit])
        loop(lo // unit, hi // unit, unrolled(lambda r: scatter_copy(visit, r).start(priority=ROW_DMA_PRIORITY)))
        loop(jnp.maximum((hi // unit) * unit, lo), hi, lambda r: scatter_copy(visit, r).start(priority=ROW_DMA_PRIORITY))

    def wait_scatter(visit):
        n = vr_ref[visit]
        loop(0, n // unit, lambda i: scatter_rows_wait())
        loop((n // unit) * unit, n, lambda r: scatter_copy(visit, r).wait())

    @pl.when((v == 0) & (f == 0))
    def _():
        xbuf[...] = jnp.zeros_like(xbuf)
        for part in range(nparts):
            start_gather(0, part)

    @pl.when((v < nv) & (f == 0))
    def _():
        wait_gather(v)
        lo, hi = _unpack_bf16_pairs(xbuf[...])
        xlo[...] = lo
        xhi[...] = hi

    @pl.when(v + 1 < nv)
    def _():
        start_gather(v + 1, f)

    @pl.when((v > 0) & (v < nv) & (f < nf - 1))
    def _():
        start_scatter(v - 1, f)

    @pl.when(v < nv)
    def _():
        wg = wg_ref[0].astype(BF16)
        wu = wu_ref[0].astype(BF16)
        xa = xlo[...]
        xb = xhi[...]
        g = jnp.dot(xa, wg[:half], preferred_element_type=F32) + jnp.dot(xb, wg[half:], preferred_element_type=F32)
        u = jnp.dot(xa, wu[:half], preferred_element_type=F32) + jnp.dot(xb, wu[half:], preferred_element_type=F32)
        hmid[f] = (_silu(g) * u).astype(BF16)
        wdb[f] = wd_ref[0].astype(BF16)

        @pl.when(f == nf - 1)
        def _():
            @pl.when(v > 0)
            def _():
                wait_scatter(v - 1)
            out = jnp.dot(hmid[0], wdb[0], preferred_element_type=F32)
            for j in range(1, hmid.shape[0]):
                out = out + jnp.dot(hmid[j], wdb[j], preferred_element_type=F32)
            acc_ref[...] = _pack_bf16_pairs(out)

            @pl.when(v == nv - 1)
            def _():
                for part in range(nparts - 1):
                    start_scatter(v, part)
                wait_scatter(v)


def _experts(visit_expert, n_visits, visit_rows, visit_start, row_slot, xn_packed, w_gate, w_up, w_down, max_visits):
    t, half = xn_packed.shape
    d = 2 * half
    fdim = w_gate.shape[2]
    nf = fdim // EXPERT_F_TILE
    ftile = lambda v, f, nv: jnp.where(v < nv[0], f, nf - 1)
    grid_spec = pltpu.PrefetchScalarGridSpec(
        num_scalar_prefetch=5,
        grid=(max_visits, nf),
        in_specs=[
            pl.BlockSpec(memory_space=pl.ANY),
            pl.BlockSpec((1, d, EXPERT_F_TILE), lambda v, f, ve, nv, vr, vs, sl: (ve[v], 0, ftile(v, f, nv))),
            pl.BlockSpec((1, d, EXPERT_F_TILE), lambda v, f, ve, nv, vr, vs, sl: (ve[v], 0, ftile(v, f, nv))),
            pl.BlockSpec((1, EXPERT_F_TILE, d), lambda v, f, ve, nv, vr, vs, sl: (ve[v], ftile(v, f, nv), 0)),
        ],
        out_specs=pl.BlockSpec(memory_space=pl.ANY),
        scratch_shapes=[
            pltpu.VMEM((GROUP_ROWS, half), jnp.uint32),
            pltpu.VMEM((GROUP_ROWS, half), BF16),
            pltpu.VMEM((GROUP_ROWS, half), BF16),
            pltpu.VMEM((nf, GROUP_ROWS, EXPERT_F_TILE), BF16),
            pltpu.VMEM((nf, EXPERT_F_TILE, d), BF16),
            pltpu.VMEM((GROUP_ROWS, half), jnp.uint32),
            pltpu.SemaphoreType.DMA(()),
            pltpu.SemaphoreType.DMA(()),
        ],
    )
    return pl.pallas_call(
        _experts_kernel,
        grid_spec=grid_spec,
        out_shape=jax.ShapeDtypeStruct((2, t, half), jnp.uint32),
        compiler_params=_cp("arbitrary", "arbitrary"),
        name="experts",
    )(visit_expert, n_visits, visit_rows, visit_start, row_slot, xn_packed, w_gate, w_up, w_down)


def _final_kernel(h_ref, o1_ref, o2_ref, wts_ref, nw_ref, y_ref):
    wts = wts_ref[...]
    unpack = lambda w: jnp.concatenate([p.astype(F32) for p in _unpack_bf16_pairs(w)], axis=-1)
    h = h_ref[...] + wts[:, 0:1] * unpack(o1_ref[0]) + wts[:, 1:2] * unpack(o2_ref[0])
    y_ref[...] = h * lax.rsqrt(jnp.mean(h * h, axis=-1, keepdims=True) + EPS) * nw_ref[...]


def _final(h1, eo, wts, norm_w, n_rows, row_block0, rows=FINAL_ROWS):
    d = h1.shape[1]
    rb = lambda i: (i + row_block0, 0)
    return pl.pallas_call(
        _final_kernel,
        grid=(n_rows // rows,),
        in_specs=[
            pl.BlockSpec((rows, d), rb),
            pl.BlockSpec((1, rows, d // 2), lambda i: (0, i + row_block0, 0)),
            pl.BlockSpec((1, rows, d // 2), lambda i: (1, i + row_block0, 0)),
            pl.BlockSpec((rows, LANES), rb),
            pl.BlockSpec((1, d), lambda i: (0, 0)),
        ],
        out_specs=pl.BlockSpec((rows, d), lambda i: (i, 0)),
        out_shape=jax.ShapeDtypeStruct((n_rows, d), F32),
        compiler_params=_cp("parallel"),
        name="final",
    )(h1, eo, eo, wts, norm_w.reshape(1, d))


def _dispatch_plan(ids, max_visits):
    flat_e = ids[:, :2].reshape(-1)
    order = jnp.argsort(flat_e, stable=True).astype(jnp.int32)
    counts = jnp.sum((flat_e[:, None] == jnp.arange(N_EXPERTS, dtype=jnp.int32)[None, :]).astype(jnp.int32), axis=0)
    tiles = (counts + GROUP_ROWS - 1) // GROUP_ROWS
    tile_end = jnp.cumsum(tiles)
    tile_start = tile_end - tiles
    group_start = jnp.cumsum(counts) - counts
    n_visits = tile_end[-1]
    visit = jnp.arange(max_visits, dtype=jnp.int32)
    visit_expert = jnp.sum((tile_end[None, :] <= jnp.minimum(visit, n_visits - 1)[:, None]).astype(jnp.int32), axis=1)
    visit_expert = jnp.minimum(visit_expert, N_EXPERTS - 1)
    tile_in_expert = visit - tile_start[visit_expert]
    rows_left = counts[visit_expert] - GROUP_ROWS * tile_in_expert
    valid = visit < n_visits
    visit_rows = jnp.where(valid, jnp.clip(rows_left, 0, GROUP_ROWS), 0).astype(jnp.int32)
    visit_start = jnp.where(valid, group_start[visit_expert] + GROUP_ROWS * tile_in_expert, 0).astype(jnp.int32)
    row_slot = jnp.pad(order, (0, DMA_UNROLL))
    return row_slot, visit_expert.astype(jnp.int32), n_visits.reshape(1).astype(jnp.int32), visit_rows, visit_start


def kernel(x_prompt, x_sample, state_delta, state_conv_qkv, state_conv_b, meta_tokens,
           norm1, w_in, w_conv_qkv, a_log, dt_bias, w_onorm, w_proj_a, w_dw, b_dw, ln_w, ln_b,
           w_proj_b, b_proj_b, w_out, norm2, w_rg, b_rg, w_re, b_re, w_e_gate, w_e_up, w_e_down,
           final_norm):
    bsz, seq, d = x_prompt.shape
    nsmp, steps, _ = x_sample.shape
    n_meta = meta_tokens.shape[0]
    depth = norm1.shape[0]
    assert depth == 1
    tp = bsz * seq
    ts = nsmp * steps
    t_all = tp + ts + n_meta
    qkv_w = 3 * DN_WIDTH
    conf_w = w_dw.shape[-1]
    hist = CONF_KERNEL - 1

    x_p = x_prompt.reshape(tp, d)
    tail_rows = pl.cdiv(ts + n_meta, TM) * TM
    x_tail = jnp.concatenate([x_sample.transpose(1, 0, 2).reshape(ts, d), meta_tokens,
                              jnp.zeros((tail_rows - ts - n_meta, d), F32)], axis=0)

    wt = jnp.transpose(w_in[0])
    o_z = qkv_w + DN_WIDTH
    o_glu = o_z + 2 * N_HEADS
    w_all = _realign_w_in(wt, o_z, o_glu - o_z, o_z + 2 * conf_w + 2 * d)
    alog_pad = jnp.pad(a_log[0], (0, LANES - N_HEADS)).reshape(1, LANES)
    dtb_pad = jnp.pad(dt_bias[0], (0, LANES - N_HEADS)).reshape(1, LANES)

    hn = _rmsnorm_bf16(x_p, x_tail, norm1[0], t_all)
    qkvz = _proj(_mm_plain_kernel, hn, w_all, 0, o_z, F32, "proj_qkvz", tn=2 * TN)
    gbeta = _proj_decay(hn, wt, o_z, alog_pad, dtb_pad)
    glu = _proj_glu(hn, w_all, o_z // TN, conf_w)
    gates = _proj(_mm_sigmoid_kernel, hn, w_all, (o_z + 2 * conf_w) // (2 * TN), 2 * d, BF16, "proj_gates", tn=2 * TN)

    wc = w_conv_qkv[0]
    meta_blk = (tp + ts) // n_meta
    qkv_meta = _prep_long(qkvz, wc, n_meta, n_meta, meta_blk, lambda i: 0, True, "prep_meta")
    rows_p = 256
    bps = seq // rows_p
    meta_halo = (tp + ts + n_meta) // 8 - 1
    qkv_p = _prep_long(qkvz, wc, tp, rows_p, 0,
                       lambda i: jnp.where(i % bps == 0, meta_halo, i * (rows_p // 8) - 1), False, "prep_prompt")
    raw_s = qkvz[tp:tp + ts, :qkv_w].reshape(steps, nsmp, qkv_w)
    st_qkv_tm = state_conv_qkv[0].transpose(1, 0, 2)
    qkv_s_tm = _prep_short(raw_s, st_qkv_tm, wc)

    zero_state = jnp.zeros((1, N_HEADS, HEAD_DIM, HEAD_DIM), F32)
    oz_meta, s_meta = _delta(qkv_meta, gbeta, qkvz, 3, zero_state, w_onorm[0], 1, 1, n_meta,
                             lambda n, ci: 0, lambda n, ci: meta_blk, lambda n: 0, "delta_meta")
    n_chunks = seq // CHUNK
    prompt_blk = lambda n, ci: n * n_chunks + ci
    oz_p, s_p = _delta(qkv_p, gbeta, qkvz, 3, s_meta, w_onorm[0], bsz, n_chunks, CHUNK,
                       prompt_blk, prompt_blk, lambda n: 0, "delta_prompt")

    cpad = 8
    to_bm = lambda a: jnp.pad(a.transpose(1, 0, 2), ((0, 0), (0, cpad - steps), (0, 0))).reshape(nsmp * cpad, a.shape[-1])
    qkv_s = to_bm(qkv_s_tm)
    gb_s = to_bm(gbeta[tp:tp + ts].reshape(steps, nsmp, LANES))
    z_s = to_bm(qkvz[tp:tp + ts, qkv_w:].reshape(steps, nsmp, DN_WIDTH))
    oz_s_bm, s_s = _delta(qkv_s, gb_s, z_s, 0, state_delta[0], w_onorm[0], nsmp, 1, cpad,
                          lambda n, ci: n, lambda n, ci: n, lambda n: n, "delta_sample")
    oz_s = oz_s_bm.reshape(nsmp, cpad, DN_WIDTH)[:, :steps].transpose(1, 0, 2).reshape(ts, DN_WIDTH)
    tail_pad = lambda w: jnp.zeros((tail_rows - ts - n_meta, w), BF16)
    oz_t = jnp.concatenate([oz_s, oz_meta, tail_pad(DN_WIDTH)], axis=0)

    zeros32 = jnp.zeros((32, conf_w), F32)
    cz_meta = _conf_long(glu, zeros32, w_dw[0], b_dw[0], ln_w[0], ln_b[0], n_meta, n_meta, meta_blk, 1,
                         lambda i: 0, "conf_meta")
    init_p = jnp.concatenate([jnp.zeros((32 - n_meta, conf_w), F32), glu[tp + ts:]], axis=0)
    rows_c = 128
    cz_p = _conf_long(glu, init_p, w_dw[0], b_dw[0], ln_w[0], ln_b[0], tp, rows_c, 0, seq // rows_c,
                      lambda i: jnp.maximum(i * (rows_c // 32) - 1, 0), "conf_prompt")
    glu_s_tm = glu[tp:tp + ts].reshape(steps, nsmp, conf_w)
    hist_s_tm = state_conv_b[0].transpose(1, 0, 2)
    cz_s = _conf_short(glu_s_tm, hist_s_tm, w_dw[0], b_dw[0], ln_w[0], ln_b[0]).reshape(ts, conf_w)
    cz_t = jnp.concatenate([cz_s, cz_meta, tail_pad(conf_w)], axis=0)

    merged = _merge(oz_p, oz_t, cz_p, cz_t, gates, w_proj_a[0].astype(BF16), w_proj_b[0].astype(BF16), b_proj_b[0])
    h1 = _out_proj(merged, w_out[0].astype(BF16), x_p, x_tail)

    w_r = jnp.pad(jnp.concatenate([w_rg[0], w_re[0]], axis=1), ((0, 0), (0, LANES - N_GROUPS - N_EXPERTS)))
    b_r = jnp.pad(jnp.concatenate([b_rg[0], b_re[0]]), (0, LANES - N_GROUPS - N_EXPERTS)).reshape(1, LANES)
    xn2, ids, wts = _router(h1, norm2[0], w_r, b_r)
    max_visits = N_EXPERTS + (2 * t_all) // GROUP_ROWS
    row_slot, visit_expert, n_visits, visit_rows, visit_start = _dispatch_plan(ids, max_visits)
    eo = _experts(visit_expert, n_visits, visit_rows, visit_start, row_slot, xn2,
                  w_e_gate[0], w_e_up[0], w_e_down[0], max_visits)
    y_p = _final(h1, eo, wts, final_norm, tp, 0)
    y_s = _final(h1, eo, wts, final_norm, ts, tp // FINAL_ROWS)

    y_prompt = y_p.reshape(bsz, seq, d)
    y_sample = y_s.reshape(steps, nsmp, d).transpose(1, 0, 2)
    new_cq_p = jnp.stack([qkvz[(b + 1) * seq - (SHORT_CONV - 1):(b + 1) * seq, :qkv_w] for b in range(bsz)])
    new_cb_p = jnp.stack([glu[(b + 1) * seq - hist:(b + 1) * seq] for b in range(bsz)])
    cq_s = jnp.concatenate([st_qkv_tm, raw_s], axis=0)[-(SHORT_CONV - 1):].transpose(1, 0, 2)
    cb_s = jnp.concatenate([hist_s_tm, glu_s_tm], axis=0)[-hist:].transpose(1, 0, 2)
    return (y_prompt, y_sample, s_p[None], new_cq_p[None], new_cb_p[None], s_s[None], cq_s[None], cb_s[None])
```
